```python
import math
import jax, jax.numpy as jnp
from jax import lax
import numpy as np

D_MODEL = 1024
BATCH = 1
SEQ = 16384
DEPTH = 2
DEC_BATCH = 32
DEC_SEQ = 8
PAST_LEN = 16384
PAGE_SIZE = 128

W_CONV = D_MODEL // 2
CONV_WIDTH = 3
W_SSM = D_MODEL // 2
SSM_GROUP = 16
N_SSM_GROUPS = W_SSM // SSM_GROUP
SSM_STATE = 64
N_HEADS = 16
HEAD_DIM = D_MODEL // N_HEADS
W_ATTN = N_HEADS * HEAD_DIM
MOBA_BLOCK = 256
MOBA_TOPK = 3
Q_CHUNK = 128
N_EVEN = (DEPTH + 1) // 2
N_ODD = DEPTH // 2
EPS = 1e-6
NEG_INF = -1e30

kernel_name = 'hybrid_conv_s5_moba_decoder_step'


def rms_norm(x, g):
    xf = x.astype(jnp.float32)
    y = xf * lax.rsqrt(jnp.mean(xf * xf, axis=-1, keepdims=True) + EPS)
    return (y * g.astype(jnp.float32)).astype(x.dtype)


def adaln(c, w, b):
    mod = (jax.nn.silu(c) @ w + b)[:, None, :]
    return jnp.split(mod, 3, axis=-1)


def alibi_slopes():
    return jnp.exp2(-8.0 * jnp.arange(1, N_HEADS + 1, dtype=jnp.float32) / N_HEADS)


def short_conv(u, buf, w, b):
    L = u.shape[1]
    full = jnp.concatenate([buf.astype(u.dtype), u], axis=1)
    y = b + sum(full[:, k:k + L] * w[k] for k in range(CONV_WIDTH))
    return y, full[:, L:]


def s5_scan(u, h0, lam_re, lam_im, log_dt, b_re, b_im, c_re, c_im, d):
    f32 = jnp.float32
    Bsz, L, _ = u.shape
    lam = lax.complex(lam_re.astype(f32), lam_im.astype(f32))
    dt = jnp.exp(log_dt.astype(f32))[:, None]
    lam_bar = jnp.exp(lam * dt)
    b_bar = ((lam_bar - 1.0) / lam)[:, :, None] * lax.complex(b_re.astype(f32), b_im.astype(f32))
    c_mat = lax.complex(c_re.astype(f32), c_im.astype(f32))
    uf = u.astype(f32)
    bu = jnp.einsum('blgh,gph->blgp', uf.reshape(Bsz, L, N_SSM_GROUPS, SSM_GROUP), b_bar)
    bu = bu.at[:, 0].add(lam_bar * h0)
    a = jnp.broadcast_to(lam_bar, bu.shape)

    def combine(e1, e2):
        return e1[0] * e2[0], e2[0] * e1[1] + e2[1]

    _, h = lax.associative_scan(combine, (a, bu), axis=1)
    y = jnp.einsum('blgp,ghp->blgh', h, c_mat).real.reshape(Bsz, L, W_SSM) + d.astype(f32) * uf
    return y.astype(u.dtype), h[:, -1]


def even_mixer(h, conv_buf, ssm_h0, w_in, conv_w, conv_b, lam_re, lam_im, log_dt,
               b_re, b_im, c_re, c_im, ssm_d, glu_w, w_out):
    proj = h @ w_in
    xa, ba, ca, za, us, zs = jnp.split(
        proj, [W_CONV, 2 * W_CONV, 3 * W_CONV, 4 * W_CONV, 4 * W_CONV + W_SSM], axis=-1)
    conv_out, new_buf = short_conv(ca * xa, conv_buf, conv_w, conv_b)
    ya = ba * conv_out * jax.nn.silu(za)
    ys, h_last = s5_scan(us, ssm_h0, lam_re, lam_im, log_dt, b_re, b_im, c_re, c_im, ssm_d)
    ys = jax.nn.gelu(ys)
    ys = ys * jax.nn.sigmoid(ys @ glu_w) * jax.nn.silu(zs)
    out = jnp.concatenate([ya, ys], axis=-1) @ w_out
    return out, new_buf, h_last


def attn_project(h, w_in):
    Bsz, L, _ = h.shape
    q, k, v, z = jnp.split(h @ w_in, 4, axis=-1)
    heads = lambda t: t.reshape(Bsz, L, N_HEADS, HEAD_DIM)
    return heads(q), heads(k), heads(v), z


def moba_attend(q, q_pos, kbar, n_past, gather_fn, k_own, v_own, own_pos, slopes):
    f32 = jnp.float32
    scale = HEAD_DIM ** -0.5
    qf = q.astype(f32)
    slope = slopes[None, None, :, None]
    qp = q_pos[None, :, None, None]
    s_own = jnp.einsum('bqhd,bshd->bqhs', qf, k_own.astype(f32)) * scale \
        - slope * (qp - own_pos).astype(f32)
    s_own = jnp.where(own_pos <= qp, s_own, NEG_INF)
    v_own = v_own.astype(f32)
    n_blk = kbar.shape[1]
    n_sel = min(MOBA_TOPK, n_blk)
    if n_sel == 0:
        p_own = jax.nn.softmax(s_own, axis=-1)
        return jnp.einsum('bqhs,bshd->bqhd', p_own, v_own).astype(q.dtype)
    gate = jnp.einsum('bqhd,bnhd->bqhn', qf, kbar.astype(f32))
    npast = n_past[None, :, None, None]
    gate = jnp.where(jnp.arange(n_blk) < npast, gate, -jnp.inf)
    _, idx = lax.top_k(gate, n_sel)
    valid = idx < npast
    kg, vg = gather_fn(idx)
    kpos = idx[..., None] * MOBA_BLOCK + jnp.arange(MOBA_BLOCK)
    s_sel = jnp.einsum('bqhd,bqhkjd->bqhkj', qf, kg.astype(f32)) * scale \
        - slope[..., None] * (qp[..., None] - kpos).astype(f32)
    s_sel = jnp.where(valid[..., None], s_sel, NEG_INF)
    Bsz, Q, H = idx.shape[:3]
    n_keys = n_sel * MOBA_BLOCK
    s_sel = s_sel.reshape(Bsz, Q, H, n_keys)
    p = jax.nn.softmax(jnp.concatenate([s_sel, s_own], axis=-1), axis=-1)
    out = jnp.einsum('bqhn,bqhnd->bqhd', p[..., :n_keys],
                     vg.reshape(Bsz, Q, H, n_keys, HEAD_DIM).astype(f32)) \
        + jnp.einsum('bqhs,bshd->bqhd', p[..., n_keys:], v_own)
    return out.astype(q.dtype)


def moba_prompt(q, k, v, slopes):
    Bsz, L, H, Dh = q.shape
    nb = -(-L // MOBA_BLOCK)
    pad = nb * MOBA_BLOCK - L
    kp = jnp.pad(k, ((0, 0), (0, pad), (0, 0), (0, 0)))
    vp = jnp.pad(v, ((0, 0), (0, pad), (0, 0), (0, 0)))
    kb = kp.reshape(Bsz, nb, MOBA_BLOCK, H, Dh)
    vb = vp.reshape(Bsz, nb, MOBA_BLOCK, H, Dh)
    kbar = jnp.mean(kb.astype(jnp.float32), axis=2)
    bidx = jnp.arange(Bsz)[:, None, None, None]
    hidx = jnp.arange(H)[None, None, :, None]

    def gather_fn(idx):
        return kb[bidx, idx, :, hidx], vb[bidx, idx, :, hidx]

    def chunk(ci):
        start = ci * Q_CHUNK
        qc = lax.dynamic_slice_in_dim(q, start, Q_CHUNK, axis=1)
        q_pos = start + jnp.arange(Q_CHUNK)
        blk_start = (start // MOBA_BLOCK) * MOBA_BLOCK
        k_own = lax.dynamic_slice_in_dim(kp, blk_start, MOBA_BLOCK, axis=1)
        v_own = lax.dynamic_slice_in_dim(vp, blk_start, MOBA_BLOCK, axis=1)
        own_pos = blk_start + jnp.arange(MOBA_BLOCK)
        return moba_attend(qc, q_pos, kbar, q_pos // MOBA_BLOCK, gather_fn, k_own, v_own, own_pos, slopes)

    out = lax.map(chunk, jnp.arange(L // Q_CHUNK))
    return out.transpose(1, 0, 2, 3, 4).reshape(Bsz, L, H, Dh)


def moba_sample(q, k_new, v_new, cache_k, cache_v, page_table, layer, slopes):
    Bsz, Lq, H, Dh = q.shape
    n_pages = page_table.shape[1]
    past = n_pages * PAGE_SIZE
    ppb = MOBA_BLOCK // PAGE_SIZE
    nb_past = past // MOBA_BLOCK
    own_start = nb_past * ppb
    n_own_cached = (n_pages - own_start) * PAGE_SIZE
    assert n_own_cached + Lq <= MOBA_BLOCK, 'new tokens must lie inside one block'
    q_pos = past + jnp.arange(Lq)
    kbar = jnp.mean(cache_k[layer, page_table[:, :own_start]].astype(jnp.float32)
                    .reshape(Bsz, nb_past, MOBA_BLOCK, H, Dh), axis=2)
    bidx = jnp.arange(Bsz)[:, None, None, None, None]
    hidx = jnp.arange(H)[None, None, :, None, None]

    def gather_fn(idx):
        logical = idx[..., None] * ppb + jnp.arange(ppb)
        phys = page_table[bidx, logical]
        shape = idx.shape + (MOBA_BLOCK, Dh)
        kg = cache_k[layer, phys, :, hidx].reshape(shape)
        vg = cache_v[layer, phys, :, hidx].reshape(shape)
        return kg, vg

    own_pt = page_table[:, own_start:]
    k_own = jnp.concatenate(
        [cache_k[layer, own_pt].reshape(Bsz, n_own_cached, H, Dh).astype(k_new.dtype), k_new], axis=1)
    v_own = jnp.concatenate(
        [cache_v[layer, own_pt].reshape(Bsz, n_own_cached, H, Dh).astype(v_new.dtype), v_new], axis=1)
    own_pos = nb_past * MOBA_BLOCK + jnp.arange(n_own_cached + Lq)
    return moba_attend(q, q_pos, kbar, q_pos // MOBA_BLOCK, gather_fn, k_own, v_own, own_pos, slopes)


def setup_inputs(seed: int = 0) -> dict:
    f32 = jnp.float32
    keys = iter(jax.random.split(jax.random.key(seed), 40))

    def normal(shape, std):
        return std * jax.random.normal(next(keys), shape, f32)

    n_pages = PAST_LEN // PAGE_SIZE
    n_pool = (DEC_BATCH * n_pages * 5) // 4
    G, P = N_SSM_GROUPS, SSM_STATE
    page_table = jax.random.permutation(next(keys), n_pool)[: DEC_BATCH * n_pages] \
        .reshape(DEC_BATCH, n_pages).astype(jnp.int32)
    return {
        'x_prompt': normal((BATCH, SEQ, D_MODEL), 1.0),
        'x_sample': normal((DEC_BATCH, DEC_SEQ, D_MODEL), 1.0),
        'state_conv': normal((N_EVEN, DEC_BATCH, CONV_WIDTH - 1, W_CONV), 1.0),
        'state_ssm_re': normal((N_EVEN, DEC_BATCH, G, P), 0.3),
        'state_ssm_im': normal((N_EVEN, DEC_BATCH, G, P), 0.3),
        'cache_k': normal((N_ODD, n_pool, PAGE_SIZE, N_HEADS, HEAD_DIM), 1.0),
        'cache_v': normal((N_ODD, n_pool, PAGE_SIZE, N_HEADS, HEAD_DIM), 1.0),
        'page_table': page_table,
        'c_prompt': normal((BATCH, D_MODEL), 1.0),
        'c_sample': normal((DEC_BATCH, D_MODEL), 1.0),
        'norm_pre': 1.0 + normal((DEPTH, D_MODEL), 0.01),
        'norm_post': 1.0 + normal((DEPTH, D_MODEL), 0.01),
        'ada_w': normal((DEPTH, D_MODEL, 3 * D_MODEL), 0.2 * D_MODEL ** -0.5),
        'ada_b': normal((DEPTH, 3 * D_MODEL), 0.01),
        'w_in_even': normal((N_EVEN, D_MODEL, 4 * W_CONV + 2 * W_SSM), D_MODEL ** -0.5),
        'conv_w': normal((N_EVEN, CONV_WIDTH, W_CONV), CONV_WIDTH ** -0.5),
        'conv_b': normal((N_EVEN, W_CONV), 0.01),
        'ssm_lambda_re': -0.5 + normal((N_EVEN, G, P), 0.01),
        'ssm_lambda_im': jnp.pi * jnp.arange(P, dtype=f32) + normal((N_EVEN, G, P), 0.01),
        'ssm_log_dt': jax.random.uniform(next(keys), (N_EVEN, G), f32, math.log(1e-3), math.log(1e-1)),
        'ssm_b_re': normal((N_EVEN, G, P, SSM_GROUP), (2 * SSM_GROUP) ** -0.5),
        'ssm_b_im': normal((N_EVEN, G, P, SSM_GROUP), (2 * SSM_GROUP) ** -0.5),
        'ssm_c_re': normal((N_EVEN, G, SSM_GROUP, P), (2 * P) ** -0.5),
        'ssm_c_im': normal((N_EVEN, G, SSM_GROUP, P), (2 * P) ** -0.5),
        'ssm_d': normal((N_EVEN, W_SSM), 1.0),
        'ssm_glu_w': normal((N_EVEN, W_SSM, W_SSM), W_SSM ** -0.5),
        'w_out_even': normal((N_EVEN, W_CONV + W_SSM, D_MODEL), (W_CONV + W_SSM) ** -0.5),
        'w_in_odd': normal((N_ODD, D_MODEL, 4 * W_ATTN), D_MODEL ** -0.5),
        'w_out_odd': normal((N_ODD, W_ATTN, D_MODEL), W_ATTN ** -0.5),
    }


def reference(x_prompt, x_sample, state_conv, state_ssm_re, state_ssm_im, cache_k, cache_v, page_table,
              c_prompt, c_sample, norm_pre, norm_post, ada_w, ada_b, w_in_even, conv_w, conv_b,
              ssm_lambda_re, ssm_lambda_im, ssm_log_dt, ssm_b_re, ssm_b_im, ssm_c_re, ssm_c_im,
              ssm_d, ssm_glu_w, w_out_even, w_in_odd, w_out_odd):
    f32 = jnp.float32
    slopes = alibi_slopes()

    def trunk(x, c, conv0, ssm0, attend):
        convs, ssms, ks, vs = [], [], [], []
        for layer in range(DEPTH):
            i = layer // 2
            shift, scale, gate = adaln(c, ada_w[layer], ada_b[layer])
            h = rms_norm(x, norm_pre[layer]) * (1.0 + scale) + shift
            if layer % 2 == 0:
                out, buf, hs = even_mixer(
                    h, conv0[i], ssm0[i], w_in_even[i], conv_w[i], conv_b[i],
                    ssm_lambda_re[i], ssm_lambda_im[i], ssm_log_dt[i], ssm_b_re[i], ssm_b_im[i],
                    ssm_c_re[i], ssm_c_im[i], ssm_d[i], ssm_glu_w[i], w_out_even[i])
                convs.append(buf)
                ssms.append(hs)
            else:
                q, k, v, z = attn_project(h, w_in_odd[i])
                o = attend(q, k, v, i)
                out = (o.reshape(z.shape) * jax.nn.silu(z)) @ w_out_odd[i]
                ks.append(k)
                vs.append(v)
            x = x + (1.0 + gate) * rms_norm(out, norm_post[layer])
        return x, jnp.stack(convs), jnp.stack(ssms), jnp.stack(ks), jnp.stack(vs)

    conv0_p = jnp.zeros((N_EVEN, x_prompt.shape[0], CONV_WIDTH - 1, W_CONV), x_prompt.dtype)
    ssm0_p = jnp.zeros((N_EVEN, x_prompt.shape[0], N_SSM_GROUPS, SSM_STATE), jnp.complex64)
    ssm0_s = lax.complex(state_ssm_re.astype(f32), state_ssm_im.astype(f32))

    y_prompt, conv_p, ssm_p, k_p, v_p = trunk(
        x_prompt, c_prompt, conv0_p, ssm0_p,
        lambda q, k, v, i: moba_prompt(q, k, v, slopes))
    y_sample, conv_s, ssm_s, k_s, v_s = trunk(
        x_sample, c_sample, state_conv, ssm0_s,
        lambda q, k, v, i: moba_sample(q, k, v, cache_k, cache_v, page_table, i, slopes))
    return (y_prompt, y_sample, conv_p, conv_s, ssm_p.real, ssm_p.imag, ssm_s.real, ssm_s.imag,
            k_p, v_p, k_s, v_s)
```

```python
import functools

import jax
import jax.numpy as jnp
from jax import lax
from jax.experimental import pallas as pl
from jax.experimental.pallas import tpu as pltpu

F32 = jnp.float32
BF16 = jnp.bfloat16
HIGHEST = lax.Precision.HIGHEST

EPS = 1e-6
NEG_INF = -1e30
N_HEADS = 16
HEAD_DIM = 64
MOBA_BLOCK = 256
MOBA_TOPK = 3
PAGE_SIZE = 128
SSM_GROUP = 16
SSM_STATE = 64
MAX_BLOCKS = 64
LANES = 128
SUBLANES = 8
VMEM_LIMIT = 48 * 1024 * 1024


def _cparams(*sem):
    return pltpu.CompilerParams(dimension_semantics=sem, vmem_limit_bytes=VMEM_LIMIT)


def _silu(x):
    return x * (1.0 / (1.0 + jnp.exp(-x)))


def _sigmoid(x):
    return 1.0 / (1.0 + jnp.exp(-x))


def _gelu_tanh(x):
    c = 0.7978845608028654
    return 0.5 * x * (1.0 + jnp.tanh(c * (x + 0.044715 * (x * x * x))))


def _rms(x, g):
    ms = jnp.mean(x * x, axis=-1, keepdims=True)
    return x * lax.rsqrt(ms + EPS) * g


def _adaln_kernel(c_ref, w_ref, b_ref, o_ref):
    c = c_ref[...]
    o_ref[0] = jnp.dot(_silu(c), w_ref[0], precision=HIGHEST,
                       preferred_element_type=F32) + b_ref[0]


def _adaln(c_all, ada_w, ada_b):
    depth, d, d3 = ada_w.shape
    r = c_all.shape[0]
    tn = 1024
    return pl.pallas_call(
        _adaln_kernel,
        grid=(depth, d3 // tn),
        in_specs=[pl.BlockSpec((r, d), lambda l, n: (0, 0)),
                  pl.BlockSpec((1, d, tn), lambda l, n: (l, 0, n)),
                  pl.BlockSpec((1, 1, tn), lambda l, n: (l, 0, n))],
        out_specs=pl.BlockSpec((1, r, tn), lambda l, n: (l, 0, n)),
        out_shape=jax.ShapeDtypeStruct((depth, r, d3), F32),
        compiler_params=_cparams("arbitrary", "arbitrary"),
    )(c_all, ada_w, ada_b.reshape(depth, 1, d3))


def _l0_in_kernel(x_ref, mod_ref, g_ref, w_ref, cw_ref, cb_ref, c0_ref,
                  ya_ref, u_ref, sz_ref, cbuf_ref, fbuf):
    ns, tl, d = x_ref.shape
    w = cw_ref.shape[1]

    @pl.when(pl.program_id(0) == 0)
    def _():
        fbuf[:, 6:8, :] = c0_ref[...]

    mod = mod_ref[...]
    shift, scale = mod[:, :, 0:d], mod[:, :, d:2 * d]
    h = _rms(x_ref[...], g_ref[...]) * (1.0 + scale) + shift
    proj = jnp.dot(h.reshape(ns * tl, d).astype(BF16), w_ref[...],
                   preferred_element_type=F32)
    xa, ba, ca = proj[:, 0:w], proj[:, w:2 * w], proj[:, 2 * w:3 * w]
    za, us, zs = proj[:, 3 * w:4 * w], proj[:, 4 * w:5 * w], proj[:, 5 * w:6 * w]
    f = (ca * xa).reshape(ns, tl, w)
    fbuf[:, 8:8 + tl, :] = f
    f1 = fbuf[:, 7:7 + tl, :]
    f2 = fbuf[:, 6:6 + tl, :]
    cw = cw_ref[...]
    conv = cb_ref[...] + cw[0:1] * f2 + cw[1:2] * f1 + cw[2:3] * f
    ya = ba.reshape(ns, tl, w) * conv * _silu(za).reshape(ns, tl, w)
    ya_ref[...] = ya
    u_ref[...] = us.reshape(ns, tl, w)
    sz_ref[...] = _silu(zs).reshape(ns, tl, w)
    tail = fbuf[:, tl + 6:tl + 8, :]
    cbuf_ref[...] = tail
    fbuf[:, 6:8, :] = tail


def _l0_in(x, mod, g, w_in, conv_w, conv_b, conv0, tl):
    b, l, d = x.shape
    w = conv_w.shape[1]
    ns = b
    return pl.pallas_call(
        _l0_in_kernel,
        grid=(l // tl,),
        in_specs=[pl.BlockSpec((ns, tl, d), lambda i: (0, i, 0)),
                  pl.BlockSpec((ns, 1, 3 * d), lambda i: (0, 0, 0)),
                  pl.BlockSpec((1, d), lambda i: (0, 0)),
                  pl.BlockSpec(w_in.shape, lambda i: (0, 0)),
                  pl.BlockSpec(conv_w.shape, lambda i: (0, 0)),
                  pl.BlockSpec((1, w), lambda i: (0, 0)),
                  pl.BlockSpec((ns, 2, w), lambda i: (0, 0, 0))],
        out_specs=[pl.BlockSpec((ns, tl, w), lambda i: (0, i, 0)),
                   pl.BlockSpec((ns, tl, w), lambda i: (0, i, 0)),
                   pl.BlockSpec((ns, tl, w), lambda i: (0, i, 0)),
                   pl.BlockSpec((ns, 2, w), lambda i: (0, 0, 0))],
        out_shape=[jax.ShapeDtypeStruct((b, l, w), F32),
                   jax.ShapeDtypeStruct((b, l, w), F32),
                   jax.ShapeDtypeStruct((b, l, w), F32),
                   jax.ShapeDtypeStruct((b, 2, w), F32)],
        scratch_shapes=[pltpu.VMEM((ns, tl + 8, w), F32)],
        compiler_params=_cparams("arbitrary"),
    )(x, mod, g.reshape(1, d), w_in, conv_w, conv_b.reshape(1, w), conv0)


def _s5_param_kernel(lr_ref, li_ref, ldt_ref, br_ref, bi_ref,
                     lbr_ref, lbi_ref, bbr_ref, bbi_ref):
    lr, li = lr_ref[...], li_ref[...]
    dt = jnp.exp(ldt_ref[...])
    mag = jnp.exp(lr * dt)
    ang = li * dt
    lbr = mag * jnp.cos(ang)
    lbi = mag * jnp.sin(ang)
    nr, ni = lbr - 1.0, lbi
    den = lr * lr + li * li
    cr = (nr * lr + ni * li) / den
    ci = (ni * lr - nr * li) / den
    br, bi = br_ref[...], bi_ref[...]
    lbr_ref[...] = lbr
    lbi_ref[...] = lbi
    bbr_ref[...] = cr * br - ci * bi
    bbi_ref[...] = cr * bi + ci * br


def _s5_params(lam_re, lam_im, log_dt, b_re, b_im):
    g, p = lam_re.shape
    hh = b_re.shape[2]
    rows = g * hh
    rep = lambda a: jnp.repeat(a, hh, axis=0)
    ldt = jnp.broadcast_to(log_dt[:, None], (g, p))
    br = b_re.transpose(0, 2, 1).reshape(rows, p)
    bi = b_im.transpose(0, 2, 1).reshape(rows, p)
    spec = pl.BlockSpec((rows, p), lambda: (0, 0))
    return pl.pallas_call(
        _s5_param_kernel,
        in_specs=[spec] * 5,
        out_specs=[spec] * 4,
        out_shape=[jax.ShapeDtypeStruct((rows, p), F32)] * 4,
    )(rep(lam_re), rep(lam_im), rep(ldt), br, bi)


def _s5_kernel(u_ref, h0_ref, lre_ref, lim_ref, wb_ref, wc_ref, d_ref,
               y_ref, hl_ref, hs, scr):
    tt = u_ref.shape[1]
    oct_w = wb_ref.shape[1]

    @pl.when(pl.program_id(1) == 0)
    def _():
        hs[...] = h0_ref[0]

    u = u_ref[0]
    ub = u.astype(BF16)
    nlt = scr.shape[0]
    for j in range(SUBLANES):
        o = j % 4
        bu = jnp.dot(ub[:, oct_w * o:oct_w * (o + 1)], wb_ref[j], preferred_element_type=F32)
        for c in range(nlt):
            scr[c, pl.ds(j, tt, stride=SUBLANES), :] = bu[:, LANES * c:LANES * (c + 1)]
    lre = [lre_ref[:, LANES * c:LANES * (c + 1)] for c in range(nlt)]
    lim = [lim_ref[:, LANES * c:LANES * (c + 1)] for c in range(nlt)]

    def body(i, hc):
        r = pl.multiple_of(i * SUBLANES, SUBLANES)
        out = []
        for c in range(nlt):
            h = lre[c] * hc[c] + lim[c] * pltpu.roll(hc[c], 4, 0) + scr[c, pl.ds(r, SUBLANES), :]
            scr[c, pl.ds(r, SUBLANES), :] = h
            out.append(h)
        return tuple(out)

    h0 = tuple(hs[:, LANES * c:LANES * (c + 1)] for c in range(nlt))
    hc = lax.fori_loop(0, tt, body, h0, unroll=8)
    h = jnp.concatenate(hc, axis=1)
    hs[...] = h
    hl_ref[0] = h
    ys = []
    for o in range(4):
        parts = [scr[c, pl.ds(part + o, tt, stride=SUBLANES), :]
                 for part in (0, 4) for c in range(nlt)]
        hcat = jnp.concatenate(parts, axis=1).astype(BF16)
        ys.append(jnp.dot(hcat, wc_ref[o], preferred_element_type=F32))
    y_ref[0] = jnp.concatenate(ys, axis=1) + d_ref[...] * u


def _s5(u, h0_tiles, lre, lim, wb, wc, d, tt):
    b, l, w = u.shape
    sw = lre.shape[1]
    return pl.pallas_call(
        _s5_kernel,
        grid=(b, l // tt),
        in_specs=[pl.BlockSpec((1, tt, w), lambda s, t: (s, t, 0)),
                  pl.BlockSpec((1, SUBLANES, sw), lambda s, t: (s, 0, 0)),
                  pl.BlockSpec(lre.shape, lambda s, t: (0, 0)),
                  pl.BlockSpec(lim.shape, lambda s, t: (0, 0)),
                  pl.BlockSpec(wb.shape, lambda s, t: (0, 0, 0)),
                  pl.BlockSpec(wc.shape, lambda s, t: (0, 0, 0)),
                  pl.BlockSpec((1, w), lambda s, t: (0, 0))],
        out_specs=[pl.BlockSpec((1, tt, w), lambda s, t: (s, t, 0)),
                   pl.BlockSpec((1, SUBLANES, sw), lambda s, t: (s, 0, 0))],
        out_shape=[jax.ShapeDtypeStruct((b, l, w), F32),
                   jax.ShapeDtypeStruct((b, SUBLANES, sw), F32)],
        scratch_shapes=[pltpu.VMEM((SUBLANES, sw), F32),
                        pltpu.VMEM((sw // LANES, tt * SUBLANES, LANES), F32)],
        compiler_params=_cparams("arbitrary", "arbitrary"),
    )(u, h0_tiles, lre, lim, wb, wc, d.reshape(1, w))


def _l0_out_kernel(x_ref, ya_ref, ys_ref, sz_ref, mod_ref, g_ref, glu_ref, w_ref, o_ref):
    ns, tl, d = x_ref.shape
    w = ya_ref.shape[2]
    rows = ns * tl
    g1 = _gelu_tanh(ys_ref[...].reshape(rows, w))
    lin = jnp.dot(g1.astype(BF16), glu_ref[...], preferred_element_type=F32)
    ys = g1 * _sigmoid(lin) * sz_ref[...].reshape(rows, w)
    cat = jnp.concatenate([ya_ref[...].reshape(rows, w).astype(BF16), ys.astype(BF16)], axis=1)
    out = jnp.dot(cat, w_ref[...], preferred_element_type=F32).reshape(ns, tl, d)
    gate = mod_ref[...][:, :, 2 * d:3 * d]
    o_ref[...] = x_ref[...] + (1.0 + gate) * _rms(out, g_ref[...])


def _l0_out(x, ya, ys, sz, mod, g, glu_w, w_out, tl):
    b, l, d = x.shape
    w = ya.shape[2]
    row = lambda width: pl.BlockSpec((b, tl, width), lambda i: (0, i, 0))
    return pl.pallas_call(
        _l0_out_kernel,
        grid=(l // tl,),
        in_specs=[row(d), row(w), row(w), row(w),
                  pl.BlockSpec((b, 1, 3 * d), lambda i: (0, 0, 0)),
                  pl.BlockSpec((1, d), lambda i: (0, 0)),
                  pl.BlockSpec(glu_w.shape, lambda i: (0, 0)),
                  pl.BlockSpec(w_out.shape, lambda i: (0, 0))],
        out_specs=row(d),
        out_shape=jax.ShapeDtypeStruct((b, l, d), F32),
        compiler_params=_cparams("arbitrary"),
    )(x, ya, ys, sz, mod, g.reshape(1, d), glu_w, w_out)


def _l1_in_kernel(prompt, x_ref, mod_ref, g_ref, w_ref, k_ref, v_ref, sz_ref, *rest):
    ns, tl, d = x_ref.shape
    rows = ns * tl
    mod = mod_ref[...]
    shift, scale = mod[:, :, 0:d], mod[:, :, d:2 * d]
    h = _rms(x_ref[...], g_ref[...]) * (1.0 + scale) + shift
    proj = jnp.dot(h.reshape(rows, d).astype(BF16), w_ref[...], preferred_element_type=F32)
    q, k, v, z = (proj[:, i * d:(i + 1) * d] for i in range(4))
    k_ref[...] = k.reshape(ns, tl, d)
    v_ref[...] = v.reshape(ns, tl, d)
    sz_ref[...] = _silu(z).reshape(ns, tl, d)
    if not prompt:
        (q_ref,) = rest
        q_ref[...] = q.reshape(ns, tl, d)
        return
    qt_ref, kaug_ref, vt_ref, kbar_ref = rest
    i = pl.program_id(0)
    qt_ref[...] = q.T.reshape(N_HEADS, 1, HEAD_DIM, rows)
    vt_ref[...] = v.T.reshape(N_HEADS, 1, HEAD_DIM, rows).astype(BF16)
    kbar_ref[0] = jnp.mean(k, axis=0, keepdims=True)
    lane = lax.broadcasted_iota(jnp.int32, (rows, LANES), 1)
    rowi = lax.broadcasted_iota(jnp.int32, (rows, LANES), 0)
    onehot = (lane - HEAD_DIM == i).astype(F32)
    tile1 = jnp.where(lane < MAX_BLOCKS, (lane == i).astype(F32),
                      jnp.where(lane < MAX_BLOCKS + 2, rowi.astype(F32), 0.0)).astype(BF16)
    for hd in range(N_HEADS):
        kt = k[:, LANES * (hd // 2):LANES * (hd // 2 + 1)]
        if hd % 2:
            kt = pltpu.roll(kt, HEAD_DIM, 1)
        tile0 = jnp.where(lane < HEAD_DIM, kt, onehot).astype(BF16)
        kaug_ref[hd, 0] = jnp.concatenate([tile0, tile1], axis=1)


def _l1_in(x, mod, g, w_in, tl, prompt):
    b, l, d = x.shape
    nt = l // tl
    row = pl.BlockSpec((b, tl, d), lambda i: (0, i, 0))
    out_specs = [row, row, row]
    out_shape = [jax.ShapeDtypeStruct((b, l, d), F32)] * 3
    if prompt:
        assert b == 1 and tl == MOBA_BLOCK and nt <= MAX_BLOCKS
        hblk = lambda r, c: pl.BlockSpec((N_HEADS, 1, r, c), lambda i: (0, i, 0, 0))
        out_specs += [hblk(HEAD_DIM, tl), hblk(tl, 2 * LANES), hblk(HEAD_DIM, tl),
                      pl.BlockSpec((1, 1, d), lambda i: (i, 0, 0))]
        out_shape += [jax.ShapeDtypeStruct((N_HEADS, nt, HEAD_DIM, tl), F32),
                      jax.ShapeDtypeStruct((N_HEADS, nt, tl, 2 * LANES), BF16),
                      jax.ShapeDtypeStruct((N_HEADS, nt, HEAD_DIM, tl), BF16),
                      jax.ShapeDtypeStruct((nt, 1, d), F32)]
    else:
        out_specs += [row]
        out_shape += [jax.ShapeDtypeStruct((b, l, d), F32)]
    return pl.pallas_call(
        functools.partial(_l1_in_kernel, prompt),
        grid=(nt,),
        in_specs=[row,
                  pl.BlockSpec((b, 1, 3 * d), lambda i: (0, 0, 0)),
                  pl.BlockSpec((1, d), lambda i: (0, 0)),
                  pl.BlockSpec(w_in.shape, lambda i: (0, 0))],
        out_specs=out_specs,
        out_shape=out_shape,
        compiler_params=_cparams("arbitrary"),
    )(x, mod, g.reshape(1, d), w_in)


def _select_topk(gate, n_past):
    nb = gate.shape[0]
    jidx = lax.broadcasted_iota(jnp.int32, gate.shape, 0)
    jf = jidx.astype(F32)
    past = jidx < n_past
    gm = jnp.where(past, gate, -jnp.inf)
    sel = None
    for _ in range(MOBA_TOPK):
        m = jnp.max(gm, axis=0, keepdims=True)
        idx = jnp.min(jnp.where(gm == m, jf, float(nb)), axis=0, keepdims=True)
        pick = jf == idx
        sel = pick if sel is None else jnp.logical_or(sel, pick)
        gm = jnp.where(pick, -jnp.inf, gm)
    return jnp.logical_and(sel, past)


def _split_bf16(a):
    hi = a.astype(BF16)
    lo = (a - hi.astype(F32)).astype(BF16)
    return hi, lo


def _gate_kernel(qt_ref, kbar_ref, slope_ref, qa_ref):
    i = pl.program_id(1)
    qt = qt_ref[0, 0]
    tq = qt.shape[1]
    gate = jnp.dot(kbar_ref[0], qt, precision=HIGHEST, preferred_element_type=F32)
    sel = _select_topk(gate, i)
    slope = slope_ref[0]
    s1 = slope[0:1, 0:1]
    jidx = lax.broadcasted_iota(jnp.int32, gate.shape, 0)
    bias = -(s1 * float(MOBA_BLOCK)) * (i - jidx).astype(F32)
    add = jnp.where(sel, bias, jnp.where(jidx == i, 0.0, NEG_INF))
    a_hi, a_lo = _split_bf16(add)
    s_hi, s_lo = _split_bf16(s1)
    tail = jnp.where(jidx == 0, s_hi.astype(F32), jnp.where(jidx == 1, s_lo.astype(F32), 0.0))
    qs = qt * (HEAD_DIM ** -0.5)
    qa = jnp.concatenate([qs, a_hi.astype(F32), a_lo.astype(F32), tail], axis=0)
    qa_ref[0, 0] = qa.astype(BF16)


def _gate(qt, kbar_h, slope_t):
    nh, nt, hd, tq = qt.shape
    return pl.pallas_call(
        _gate_kernel,
        grid=(nh, nt),
        in_specs=[pl.BlockSpec((1, 1, hd, tq), lambda h, i: (h, i, 0, 0)),
                  pl.BlockSpec((1, MAX_BLOCKS, hd), lambda h, i: (h, 0, 0)),
                  pl.BlockSpec((1, SUBLANES, LANES), lambda h, i: (h, 0, 0))],
        out_specs=pl.BlockSpec((1, 1, 2 * LANES, tq), lambda h, i: (h, i, 0, 0)),
        out_shape=jax.ShapeDtypeStruct((nh, nt, 2 * LANES, tq), BF16),
        compiler_params=_cparams("arbitrary", "arbitrary"),
    )(qt, kbar_h, slope_t)


def _attn_kernel(qa_ref, kaug_ref, vt_ref, o_ref):
    i = pl.program_id(1)
    qa = qa_ref[0, 0]
    tq = qa.shape[1]

    def scores(j):
        return jnp.dot(kaug_ref[0, j], qa, preferred_element_type=F32)

    def update(j, s, carry):
        m, l, acc = carry
        m_new = jnp.maximum(m, jnp.max(s, axis=0, keepdims=True))
        p = jnp.exp(s - m_new)
        alpha = jnp.exp(m - m_new)
        l = alpha * l + jnp.sum(p, axis=0, keepdims=True)
        acc = alpha * acc + jnp.dot(vt_ref[0, j], p.astype(BF16), preferred_element_type=F32)
        return m_new, l, acc

    init = (jnp.full((1, tq), -jnp.inf, F32), jnp.zeros((1, tq), F32),
            jnp.zeros((HEAD_DIM, tq), F32))
    carry = lax.fori_loop(0, i, lambda j, c: update(j, scores(j), c), init)
    s = scores(i)
    key = lax.broadcasted_iota(jnp.int32, s.shape, 0)
    qry = lax.broadcasted_iota(jnp.int32, s.shape, 1)
    m, l, acc = update(i, jnp.where(key <= qry, s, NEG_INF), carry)
    o_ref[0, 0] = acc / l


def _attn(qa, kaug, vt):
    nh, nt, f, tq = qa.shape
    return pl.pallas_call(
        _attn_kernel,
        grid=(nh, nt),
        in_specs=[pl.BlockSpec((1, 1, f, tq), lambda h, i: (h, i, 0, 0)),
                  pl.BlockSpec((1, nt, tq, f), lambda h, i: (h, 0, 0, 0)),
                  pl.BlockSpec((1, nt, HEAD_DIM, tq), lambda h, i: (h, 0, 0, 0))],
        out_specs=pl.BlockSpec((1, 1, HEAD_DIM, tq), lambda h, i: (h, i, 0, 0)),
        out_shape=jax.ShapeDtypeStruct((nh, nt, HEAD_DIM, tq), F32),
        compiler_params=_cparams("arbitrary", "arbitrary"),
    )(qa, kaug, vt)


def _fold_heads(pv):
    rows, d = pv.shape
    r = lax.broadcasted_iota(jnp.int32, (rows, LANES), 0) // SUBLANES
    c = lax.broadcasted_iota(jnp.int32, (rows, LANES), 1) // HEAD_DIM
    out = jnp.zeros((rows, LANES), F32)
    for t in range(d // LANES):
        out = out + jnp.where(r == 2 * t + c, pv[:, LANES * t:LANES * (t + 1)], 0.0)
    return out


def _sample_attn_kernel(pt_ref, k0_ref, k1_ref, v0_ref, v1_ref, qs_ref, qf_ref, kn_ref, vn_ref,
                        slope_ref, qoff_ref, o_ref, kbar_s, m_s, l_s, o_s):
    j = pl.program_id(1)
    nb = kbar_s.shape[0]
    qs = qs_ref[0]
    slope = slope_ref[...]
    n = slope.shape[1]

    def pv_fold(p, v):
        pv = lax.dot_general(p.astype(BF16), v.astype(BF16), (((0,), (0,)), ((), ())),
                             preferred_element_type=F32)
        return _fold_heads(pv)

    kb = jnp.concatenate([k0_ref[0, 0], k1_ref[0, 0]], axis=0)
    vb = jnp.concatenate([v0_ref[0, 0], v1_ref[0, 0]], axis=0)
    cpos = lax.broadcasted_iota(jnp.int32, (MOBA_BLOCK, 1), 0).astype(F32)
    s = jnp.dot(kb.astype(BF16), qs, preferred_element_type=F32) + slope * cpos
    m = jnp.max(s, axis=0, keepdims=True)
    p = jnp.exp(s - m)
    kbar_s[pl.ds(j, 1), :] = jnp.mean(kb, axis=0, keepdims=True)
    m_s[pl.ds(j, 1), :] = m
    l_s[pl.ds(j, 1), :] = jnp.sum(p, axis=0, keepdims=True)
    o_s[j] = pv_fold(p, vb)

    @pl.when(j == nb - 1)
    def _():
        nbs = kbar_s.shape[0]
        gate = jnp.dot(kbar_s[...], qf_ref[0], precision=HIGHEST, preferred_element_type=F32)
        sel = _select_topk(gate, nb)
        jidx = lax.broadcasted_iota(jnp.int32, (nbs, n), 0)
        qoff = qoff_ref[...]
        past_len = float(nb * MOBA_BLOCK)
        mj = m_s[...] - slope * (past_len + qoff - (jidx * MOBA_BLOCK).astype(F32))
        mj = jnp.where(sel, mj, -jnp.inf)
        nq = kn_ref.shape[1]
        kpos = lax.broadcasted_iota(jnp.int32, (nq, 1), 0).astype(F32)
        s_own = jnp.dot(kn_ref[0].astype(BF16), qs, preferred_element_type=F32) \
            - slope * (qoff - kpos)
        s_own = jnp.where(kpos <= qoff, s_own, NEG_INF)
        m_own = jnp.max(s_own, axis=0, keepdims=True)
        p_own = jnp.exp(s_own - m_own)
        l_own = jnp.sum(p_own, axis=0, keepdims=True)
        o_own = pv_fold(p_own, vn_ref[0])
        m_tot = jnp.maximum(jnp.max(mj, axis=0, keepdims=True), m_own)
        wj = jnp.where(sel, jnp.exp(mj - m_tot), 0.0)
        w_own = jnp.exp(m_own - m_tot)
        l_tot = jnp.sum(wj * l_s[...], axis=0, keepdims=True) + w_own * l_own
        w_pad = jnp.concatenate([wj / l_tot, jnp.zeros((LANES - nbs, n), F32)], axis=0)
        row_w = lax.broadcasted_iota(jnp.int32, w_pad.shape, 0)
        w_t = jnp.where(row_w == nbs, w_own / l_tot, w_pad).T
        lane_w = lax.broadcasted_iota(jnp.int32, w_t.shape, 1)

        def column(jj):
            return jnp.sum(jnp.where(lane_w == jj, w_t, 0.0), axis=1, keepdims=True)

        acc = lax.fori_loop(0, nb, lambda jj, a: a + column(jj) * o_s[jj],
                            column(nbs) * o_own)
        lane = lax.broadcasted_iota(jnp.int32, (nq, LANES), 1)
        tiles = []
        for t in range(N_HEADS // 2):
            ev = acc[SUBLANES * 2 * t:SUBLANES * (2 * t + 1), :]
            od = acc[SUBLANES * (2 * t + 1):SUBLANES * (2 * t + 2), :]
            tiles.append(jnp.where(lane < HEAD_DIM, ev, od))
        o_ref[0] = jnp.concatenate(tiles, axis=1)


def _sample_attn(page_table, cache_k, cache_v, layer, qs_bd, qf_bd, k_new, v_new, slope_row, qoff_row):
    b, n_pages = page_table.shape
    ppb = MOBA_BLOCK // PAGE_SIZE
    nb = n_pages // ppb
    assert n_pages % ppb == 0 and ppb == 2 and nb <= MAX_BLOCKS
    lq, d = k_new.shape[1], k_new.shape[2]
    n = qs_bd.shape[2]
    assert lq == SUBLANES and n == LANES
    page = lambda off: pl.BlockSpec(
        (1, 1, PAGE_SIZE, d), lambda s, j, pt: (layer, pt[s, ppb * j + off], 0, 0))
    per_seq = lambda r, c: pl.BlockSpec((1, r, c), lambda s, j, pt: (s, 0, 0))
    const = lambda: pl.BlockSpec((1, n), lambda s, j, pt: (0, 0))
    grid_spec = pltpu.PrefetchScalarGridSpec(
        num_scalar_prefetch=1,
        grid=(b, nb),
        in_specs=[page(0), page(1), page(0), page(1),
                  per_seq(d, n), per_seq(d, n), per_seq(lq, d), per_seq(lq, d),
                  const(), const()],
        out_specs=per_seq(lq, d),
        scratch_shapes=[pltpu.VMEM((nb, d), F32),
                        pltpu.VMEM((nb, n), F32),
                        pltpu.VMEM((nb, n), F32),
                        pltpu.VMEM((nb, n, LANES), F32)],
    )
    return pl.pallas_call(
        _sample_attn_kernel,
        grid_spec=grid_spec,
        out_shape=jax.ShapeDtypeStruct((b, lq, d), F32),
        compiler_params=_cparams("arbitrary", "arbitrary"),
    )(page_table, cache_k, cache_k, cache_v, cache_v, qs_bd, qf_bd, k_new, v_new,
      slope_row, qoff_row)


def _l1_out_kernel(transposed, x_ref, o_ref, sz_ref, mod_ref, g_ref, w_ref, y_ref):
    ns, tl, d = x_ref.shape
    rows = ns * tl
    if transposed:
        o = o_ref[...].reshape(d, rows).T
    else:
        o = o_ref[...].reshape(rows, d)
    a = (o * sz_ref[...].reshape(rows, d)).astype(BF16)
    out = jnp.dot(a, w_ref[...], preferred_element_type=F32).reshape(ns, tl, d)
    gate = mod_ref[...][:, :, 2 * d:3 * d]
    y_ref[...] = x_ref[...] + (1.0 + gate) * _rms(out, g_ref[...])


def _l1_out(x, o, sz, mod, g, w_out, tl, transposed):
    b, l, d = x.shape
    row = pl.BlockSpec((b, tl, d), lambda i: (0, i, 0))
    if transposed:
        o_spec = pl.BlockSpec((N_HEADS, 1, HEAD_DIM, tl), lambda i: (0, i, 0, 0))
    else:
        o_spec = row
    return pl.pallas_call(
        functools.partial(_l1_out_kernel, transposed),
        grid=(l // tl,),
        in_specs=[row, o_spec, row,
                  pl.BlockSpec((b, 1, 3 * d), lambda i: (0, 0, 0)),
                  pl.BlockSpec((1, d), lambda i: (0, 0)),
                  pl.BlockSpec(w_out.shape, lambda i: (0, 0))],
        out_specs=row,
        out_shape=jax.ShapeDtypeStruct((b, l, d), F32),
        compiler_params=_cparams("arbitrary"),
    )(x, o, sz, mod, g.reshape(1, d), w_out)


def _s5_weights(lbr, lbi, bbr, bbi, c_re, c_im):
    g = c_re.shape[0]
    hh, p = SSM_GROUP, SSM_STATE
    oct_n = g // 8
    lam_r = lbr.reshape(g, hh, p)[:, 0, :].reshape(oct_n, 8 * p)
    lam_i = lbi.reshape(g, hh, p)[:, 0, :].reshape(oct_n, 8 * p)
    lre = jnp.concatenate([lam_r, lam_r], axis=0)
    lim = jnp.concatenate([-lam_i, lam_i], axis=0)
    eye = jnp.eye(8, dtype=F32)

    def in_w(bb):
        bb = bb.reshape(oct_n, 8, hh, p)
        return jnp.einsum('oghp,gk->oghkp', bb, eye).reshape(oct_n, 8 * hh, 8 * p)

    def out_w(cc):
        cc = cc.reshape(oct_n, 8, hh, p)
        return jnp.einsum('oghp,gk->ogpkh', cc, eye).reshape(oct_n, 8 * p, 8 * hh)

    wb = jnp.concatenate([in_w(bbr), in_w(bbi)], axis=0).astype(BF16)
    wc = jnp.concatenate([out_w(c_re), -out_w(c_im)], axis=1).astype(BF16)
    return lre, lim, wb, wc


def _state_tiles(re, im):
    b = re.shape[0]
    return jnp.concatenate([re.reshape(b, 4, -1), im.reshape(b, 4, -1)], axis=1)


def _block_diag_q(q):
    b, lq, _ = q.shape
    qh = q.reshape(b, lq, N_HEADS, HEAD_DIM)
    eye = jnp.eye(N_HEADS, dtype=q.dtype)
    return jnp.einsum('bqhd,hk->bhdkq', qh, eye).reshape(b, N_HEADS * HEAD_DIM, N_HEADS * lq)


def kernel(x_prompt, x_sample, state_conv, state_ssm_re, state_ssm_im, cache_k, cache_v, page_table,
           c_prompt, c_sample, norm_pre, norm_post, ada_w, ada_b, w_in_even, conv_w, conv_b,
           ssm_lambda_re, ssm_lambda_im, ssm_log_dt, ssm_b_re, ssm_b_im, ssm_c_re, ssm_c_im,
           ssm_d, ssm_glu_w, w_out_even, w_in_odd, w_out_odd):
    bp, lp, d = x_prompt.shape
    bs, ls, _ = x_sample.shape
    g, p = ssm_lambda_re.shape[1], ssm_lambda_re.shape[2]
    n_pool = cache_k.shape[1]
    assert bp == 1 and ls == SUBLANES and d == N_HEADS * HEAD_DIM and g == 32 and p == SSM_STATE

    n_c = bp + bs
    c_all = jnp.concatenate(
        [c_prompt, c_sample, jnp.zeros((-n_c % SUBLANES, d), F32)], axis=0)
    mod = _adaln(c_all, ada_w, ada_b)
    mod_p = [mod[l, 0:bp][:, None, :] for l in range(2)]
    mod_s = [mod[l, bp:n_c][:, None, :] for l in range(2)]

    lbr, lbi, bbr, bbi = _s5_params(ssm_lambda_re[0], ssm_lambda_im[0], ssm_log_dt[0],
                                    ssm_b_re[0], ssm_b_im[0])
    lre, lim, wb, wc = _s5_weights(lbr, lbi, bbr, bbi, ssm_c_re[0], ssm_c_im[0])
    w_in0 = w_in_even[0].astype(BF16)
    glu_w = ssm_glu_w[0].astype(BF16)
    w_out0 = w_out_even[0].astype(BF16)
    wcv = conv_w.shape[2]

    def layer0(x, mods, conv0, h0_tiles, tl, tt):
        ya, u, sz, cbuf = _l0_in(x, mods, norm_pre[0], w_in0, conv_w[0], conv_b[0], conv0, tl)
        ys, h_last = _s5(u, h0_tiles, lre, lim, wb, wc, ssm_d[0], tt)
        x1 = _l0_out(x, ya, ys, sz, mods, norm_post[0], glu_w, w_out0, tl)
        b = x.shape[0]
        h_re = h_last[:, 0:4].reshape(b, g, p)
        h_im = h_last[:, 4:8].reshape(b, g, p)
        return x1, cbuf, h_re, h_im

    x1_p, conv_p, hre_p, him_p = layer0(
        x_prompt, mod_p[0], jnp.zeros((bp, 2, wcv), F32),
        jnp.zeros((bp, SUBLANES, 2 * g * p // SUBLANES), F32), 512, 256)
    x1_s, conv_s, hre_s, him_s = layer0(
        x_sample, mod_s[0], state_conv[0],
        _state_tiles(state_ssm_re[0], state_ssm_im[0]), ls, ls)

    w_in1 = w_in_odd[0].astype(BF16)
    w_out1 = w_out_odd[0].astype(BF16)
    slopes = jnp.exp2(-8.0 * jnp.arange(1, N_HEADS + 1, dtype=F32) / N_HEADS)

    k_p, v_p, sz_p, qt, kaug, vt, kbar = _l1_in(x1_p, mod_p[1], norm_pre[1], w_in1, MOBA_BLOCK, True)
    nt = lp // MOBA_BLOCK
    kbar_h = kbar.reshape(nt, N_HEADS, HEAD_DIM).transpose(1, 0, 2)
    kbar_h = jnp.pad(kbar_h, ((0, 0), (0, MAX_BLOCKS - nt), (0, 0)))
    slope_t = jnp.broadcast_to(slopes[:, None, None], (N_HEADS, SUBLANES, LANES))
    qa = _gate(qt, kbar_h, slope_t)
    o_t = _attn(qa, kaug, vt)
    y_p = _l1_out(x1_p, o_t, sz_p, mod_p[1], norm_post[1], w_out1, MOBA_BLOCK, True)

    k_s, v_s, sz_s, q_s = _l1_in(x1_s, mod_s[1], norm_pre[1], w_in1, ls, False)
    qf_bd = _block_diag_q(q_s)
    qs_bd = (qf_bd * (HEAD_DIM ** -0.5)).astype(BF16)
    slope_row = jnp.repeat(slopes, ls)[None, :]
    qoff_row = jnp.tile(jnp.arange(ls, dtype=F32), N_HEADS)[None, :]
    ck = cache_k.reshape(cache_k.shape[0], n_pool, PAGE_SIZE, d)
    cv = cache_v.reshape(cache_v.shape[0], n_pool, PAGE_SIZE, d)
    o_s = _sample_attn(page_table, ck, cv, 0, qs_bd, qf_bd, k_s, v_s, slope_row, qoff_row)
    y_s = _l1_out(x1_s, o_s, sz_s, mod_s[1], norm_post[1], w_out1, ls, False)

    heads = lambda t: t.reshape(1, t.shape[0], t.shape[1], N_HEADS, HEAD_DIM)
    return (y_p, y_s, conv_p[None], conv_s[None],
            hre_p[None], him_p[None], hre_s[None], him_s[None],
            heads(k_p), heads(v_p), heads(k_s), heads(v_s))
```

```python
import functools

import jax
import jax.numpy as jnp
from jax import lax
from jax.experimental import pallas as pl
from jax.experimental.pallas import tpu as pltpu

F32 = jnp.float32
BF16 = jnp.bfloat16
HIGHEST = lax.Precision.HIGHEST

EPS = 1e-6
NEG_INF = -1e30
N_HEADS = 16
HEAD_DIM = 64
MOBA_BLOCK = 256
MOBA_TOPK = 3
PAGE_SIZE = 128
SSM_GROUP = 16
SSM_STATE = 64
MAX_BLOCKS = 64
KV_GROUP = 4
Q_TILE = 512
V_ROWS = 80
LOG2E = 1.4426950408889634
LANES = 128
SUBLANES = 8
VMEM_LIMIT = 48 * 1024 * 1024


def _cparams(*sem):
    return pltpu.CompilerParams(dimension_semantics=sem, vmem_limit_bytes=VMEM_LIMIT)


def _silu(x):
    return x * (1.0 / (1.0 + jnp.exp(-x)))


def _sigmoid(x):
    return 1.0 / (1.0 + jnp.exp(-x))


def _gelu_tanh(x):
    c = 0.7978845608028654
    return 0.5 * x * (1.0 + jnp.tanh(c * (x + 0.044715 * (x * x * x))))


def _rms(x, g):
    ms = jnp.mean(x * x, axis=-1, keepdims=True)
    return x * lax.rsqrt(ms + EPS) * g


def _adaln_kernel(c_ref, w_ref, b_ref, o_ref):
    c = c_ref[...]
    o_ref[0] = jnp.dot(_silu(c), w_ref[0], precision=HIGHEST,
                       preferred_element_type=F32) + b_ref[0]


def _adaln(c_all, ada_w, ada_b):
    depth, d, d3 = ada_w.shape
    r = c_all.shape[0]
    tn = 1024
    return pl.pallas_call(
        _adaln_kernel,
        grid=(depth, d3 // tn),
        in_specs=[pl.BlockSpec((r, d), lambda l, n: (0, 0)),
                  pl.BlockSpec((1, d, tn), lambda l, n: (l, 0, n)),
                  pl.BlockSpec((1, 1, tn), lambda l, n: (l, 0, n))],
        out_specs=pl.BlockSpec((1, r, tn), lambda l, n: (l, 0, n)),
        out_shape=jax.ShapeDtypeStruct((depth, r, d3), F32),
        compiler_params=_cparams("arbitrary", "arbitrary"),
    )(c_all, ada_w, ada_b.reshape(depth, 1, d3))


def _l0_in_kernel(x_ref, mod_ref, g_ref, w_ref, cw_ref, cb_ref, c0_ref,
                  ya_ref, u_ref, sz_ref, cbuf_ref, fbuf):
    ns, tl, d = x_ref.shape
    w = cw_ref.shape[1]

    @pl.when(pl.program_id(0) == 0)
    def _():
        fbuf[:, 6:8, :] = c0_ref[...]

    mod = mod_ref[...]
    shift, scale = mod[:, :, 0:d], mod[:, :, d:2 * d]
    h = _rms(x_ref[...], g_ref[...]) * (1.0 + scale) + shift
    proj = jnp.dot(h.reshape(ns * tl, d).astype(BF16), w_ref[...],
                   preferred_element_type=F32)
    xa, ba, ca = proj[:, 0:w], proj[:, w:2 * w], proj[:, 2 * w:3 * w]
    za, us, zs = proj[:, 3 * w:4 * w], proj[:, 4 * w:5 * w], proj[:, 5 * w:6 * w]
    f = (ca * xa).reshape(ns, tl, w)
    fbuf[:, 8:8 + tl, :] = f
    f1 = fbuf[:, 7:7 + tl, :]
    f2 = fbuf[:, 6:6 + tl, :]
    cw = cw_ref[...]
    conv = cb_ref[...] + cw[0:1] * f2 + cw[1:2] * f1 + cw[2:3] * f
    ya = ba.reshape(ns, tl, w) * conv * _silu(za).reshape(ns, tl, w)
    ya_ref[...] = ya
    u_ref[...] = us.reshape(ns, tl, w)
    sz_ref[...] = _silu(zs).reshape(ns, tl, w)
    tail = fbuf[:, tl + 6:tl + 8, :]
    cbuf_ref[...] = tail
    fbuf[:, 6:8, :] = tail


def _l0_in(x, mod, g, w_in, conv_w, conv_b, conv0, tl):
    b, l, d = x.shape
    w = conv_w.shape[1]
    ns = b
    return pl.pallas_call(
        _l0_in_kernel,
        grid=(l // tl,),
        in_specs=[pl.BlockSpec((ns, tl, d), lambda i: (0, i, 0)),
                  pl.BlockSpec((ns, 1, 3 * d), lambda i: (0, 0, 0)),
                  pl.BlockSpec((1, d), lambda i: (0, 0)),
                  pl.BlockSpec(w_in.shape, lambda i: (0, 0)),
                  pl.BlockSpec(conv_w.shape, lambda i: (0, 0)),
                  pl.BlockSpec((1, w), lambda i: (0, 0)),
                  pl.BlockSpec((ns, 2, w), lambda i: (0, 0, 0))],
        out_specs=[pl.BlockSpec((ns, tl, w), lambda i: (0, i, 0)),
                   pl.BlockSpec((ns, tl, w), lambda i: (0, i, 0)),
                   pl.BlockSpec((ns, tl, w), lambda i: (0, i, 0)),
                   pl.BlockSpec((ns, 2, w), lambda i: (0, 0, 0))],
        out_shape=[jax.ShapeDtypeStruct((b, l, w), F32),
                   jax.ShapeDtypeStruct((b, l, w), F32),
                   jax.ShapeDtypeStruct((b, l, w), F32),
                   jax.ShapeDtypeStruct((b, 2, w), F32)],
        scratch_shapes=[pltpu.VMEM((ns, tl + 8, w), F32)],
        compiler_params=_cparams("arbitrary"),
    )(x, mod, g.reshape(1, d), w_in, conv_w, conv_b.reshape(1, w), conv0)


def _s5_param_kernel(lr_ref, li_ref, ldt_ref, br_ref, bi_ref,
                     lbr_ref, lbi_ref, bbr_ref, bbi_ref):
    lr, li = lr_ref[...], li_ref[...]
    dt = jnp.exp(ldt_ref[...])
    mag = jnp.exp(lr * dt)
    ang = li * dt
    lbr = mag * jnp.cos(ang)
    lbi = mag * jnp.sin(ang)
    nr, ni = lbr - 1.0, lbi
    den = lr * lr + li * li
    cr = (nr * lr + ni * li) / den
    ci = (ni * lr - nr * li) / den
    br, bi = br_ref[...], bi_ref[...]
    lbr_ref[...] = lbr
    lbi_ref[...] = lbi
    bbr_ref[...] = cr * br - ci * bi
    bbi_ref[...] = cr * bi + ci * br


def _s5_params(lam_re, lam_im, log_dt, b_re, b_im):
    g, p = lam_re.shape
    hh = b_re.shape[2]
    rows = g * hh
    rep = lambda a: jnp.repeat(a, hh, axis=0)
    ldt = jnp.broadcast_to(log_dt[:, None], (g, p))
    br = b_re.transpose(0, 2, 1).reshape(rows, p)
    bi = b_im.transpose(0, 2, 1).reshape(rows, p)
    spec = pl.BlockSpec((rows, p), lambda: (0, 0))
    return pl.pallas_call(
        _s5_param_kernel,
        in_specs=[spec] * 5,
        out_specs=[spec] * 4,
        out_shape=[jax.ShapeDtypeStruct((rows, p), F32)] * 4,
    )(rep(lam_re), rep(lam_im), rep(ldt), br, bi)


def _s5_kernel(u_ref, h0_ref, lre_ref, lim_ref, wb_ref, wc_ref, d_ref,
               y_ref, hl_ref, hs, scr):
    tt = u_ref.shape[1]
    oct_w = wb_ref.shape[1]

    @pl.when(pl.program_id(1) == 0)
    def _():
        hs[...] = h0_ref[0]

    u = u_ref[0]
    ub = u.astype(BF16)
    nlt = scr.shape[0]
    for j in range(SUBLANES):
        o = j % 4
        bu = jnp.dot(ub[:, oct_w * o:oct_w * (o + 1)], wb_ref[j], preferred_element_type=F32)
        for c in range(nlt):
            scr[c, pl.ds(j, tt, stride=SUBLANES), :] = bu[:, LANES * c:LANES * (c + 1)]
    lre = [lre_ref[:, LANES * c:LANES * (c + 1)] for c in range(nlt)]
    lim = [lim_ref[:, LANES * c:LANES * (c + 1)] for c in range(nlt)]

    def body(i, hc):
        r = pl.multiple_of(i * SUBLANES, SUBLANES)
        out = []
        for c in range(nlt):
            h = lre[c] * hc[c] + lim[c] * pltpu.roll(hc[c], 4, 0) + scr[c, pl.ds(r, SUBLANES), :]
            scr[c, pl.ds(r, SUBLANES), :] = h
            out.append(h)
        return tuple(out)

    h0 = tuple(hs[:, LANES * c:LANES * (c + 1)] for c in range(nlt))
    hc = lax.fori_loop(0, tt, body, h0, unroll=8)
    h = jnp.concatenate(hc, axis=1)
    hs[...] = h
    hl_ref[0] = h
    ys = []
    for o in range(4):
        parts = [scr[c, pl.ds(part + o, tt, stride=SUBLANES), :]
                 for part in (0, 4) for c in range(nlt)]
        hcat = jnp.concatenate(parts, axis=1).astype(BF16)
        ys.append(jnp.dot(hcat, wc_ref[o], preferred_element_type=F32))
    y_ref[0] = jnp.concatenate(ys, axis=1) + d_ref[...] * u


def _s5(u, h0_tiles, lre, lim, wb, wc, d, tt):
    b, l, w = u.shape
    sw = lre.shape[1]
    return pl.pallas_call(
        _s5_kernel,
        grid=(b, l // tt),
        in_specs=[pl.BlockSpec((1, tt, w), lambda s, t: (s, t, 0)),
                  pl.BlockSpec((1, SUBLANES, sw), lambda s, t: (s, 0, 0)),
                  pl.BlockSpec(lre.shape, lambda s, t: (0, 0)),
                  pl.BlockSpec(lim.shape, lambda s, t: (0, 0)),
                  pl.BlockSpec(wb.shape, lambda s, t: (0, 0, 0)),
                  pl.BlockSpec(wc.shape, lambda s, t: (0, 0, 0)),
                  pl.BlockSpec((1, w), lambda s, t: (0, 0))],
        out_specs=[pl.BlockSpec((1, tt, w), lambda s, t: (s, t, 0)),
                   pl.BlockSpec((1, SUBLANES, sw), lambda s, t: (s, 0, 0))],
        out_shape=[jax.ShapeDtypeStruct((b, l, w), F32),
                   jax.ShapeDtypeStruct((b, SUBLANES, sw), F32)],
        scratch_shapes=[pltpu.VMEM((SUBLANES, sw), F32),
                        pltpu.VMEM((sw // LANES, tt * SUBLANES, LANES), F32)],
        compiler_params=_cparams("arbitrary", "arbitrary"),
    )(u, h0_tiles, lre, lim, wb, wc, d.reshape(1, w))


def _l0_out_kernel(x_ref, ya_ref, ys_ref, sz_ref, mod_ref, g_ref, glu_ref, w_ref, o_ref):
    ns, tl, d = x_ref.shape
    w = ya_ref.shape[2]
    rows = ns * tl
    g1 = _gelu_tanh(ys_ref[...].reshape(rows, w))
    lin = jnp.dot(g1.astype(BF16), glu_ref[...], preferred_element_type=F32)
    ys = g1 * _sigmoid(lin) * sz_ref[...].reshape(rows, w)
    cat = jnp.concatenate([ya_ref[...].reshape(rows, w).astype(BF16), ys.astype(BF16)], axis=1)
    out = jnp.dot(cat, w_ref[...], preferred_element_type=F32).reshape(ns, tl, d)
    gate = mod_ref[...][:, :, 2 * d:3 * d]
    o_ref[...] = x_ref[...] + (1.0 + gate) * _rms(out, g_ref[...])


def _l0_out(x, ya, ys, sz, mod, g, glu_w, w_out, tl):
    b, l, d = x.shape
    w = ya.shape[2]
    row = lambda width: pl.BlockSpec((b, tl, width), lambda i: (0, i, 0))
    return pl.pallas_call(
        _l0_out_kernel,
        grid=(l // tl,),
        in_specs=[row(d), row(w), row(w), row(w),
                  pl.BlockSpec((b, 1, 3 * d), lambda i: (0, 0, 0)),
                  pl.BlockSpec((1, d), lambda i: (0, 0)),
                  pl.BlockSpec(glu_w.shape, lambda i: (0, 0)),
                  pl.BlockSpec(w_out.shape, lambda i: (0, 0))],
        out_specs=row(d),
        out_shape=jax.ShapeDtypeStruct((b, l, d), F32),
        compiler_params=_cparams("arbitrary"),
    )(x, ya, ys, sz, mod, g.reshape(1, d), glu_w, w_out)


def _l1_in_kernel(prompt, x_ref, mod_ref, g_ref, w_ref, k_ref, v_ref, sz_ref, *rest):
    ns, tl, d = x_ref.shape
    rows = ns * tl
    mod = mod_ref[...]
    shift, scale = mod[:, :, 0:d], mod[:, :, d:2 * d]
    h = _rms(x_ref[...], g_ref[...]) * (1.0 + scale) + shift
    proj = jnp.dot(h.reshape(rows, d).astype(BF16), w_ref[...], preferred_element_type=F32)
    q, k, v, z = (proj[:, i * d:(i + 1) * d] for i in range(4))
    k_ref[...] = k.reshape(ns, tl, d)
    v_ref[...] = v.reshape(ns, tl, d)
    sz_ref[...] = _silu(z).reshape(ns, tl, d)
    if not prompt:
        (q_ref,) = rest
        q_ref[...] = q.reshape(ns, tl, d)
        return
    qt_ref, kaug_ref, vt_ref, kbar_ref = rest
    i = pl.program_id(0)
    qt_ref[...] = q.T.reshape(N_HEADS, HEAD_DIM, rows)
    ones_rows = (lax.broadcasted_iota(jnp.int32, (N_HEADS, V_ROWS - HEAD_DIM, rows), 1) == 0)
    vt = jnp.concatenate([v.T.reshape(N_HEADS, HEAD_DIM, rows), ones_rows.astype(F32)], axis=1)
    vt_ref[...] = vt.reshape(N_HEADS, 1, V_ROWS, rows).astype(BF16)
    kbar_ref[0] = jnp.mean(k, axis=0, keepdims=True)
    lane = lax.broadcasted_iota(jnp.int32, (rows, LANES), 1)
    rowi = lax.broadcasted_iota(jnp.int32, (rows, LANES), 0)
    onehot = (lane - HEAD_DIM == i).astype(F32)
    tile1 = jnp.where(lane < MAX_BLOCKS, (lane == i).astype(F32),
                      jnp.where(lane < MAX_BLOCKS + 2, rowi.astype(F32), 0.0)).astype(BF16)
    for hd in range(N_HEADS):
        kt = k[:, LANES * (hd // 2):LANES * (hd // 2 + 1)]
        if hd % 2:
            kt = pltpu.roll(kt, HEAD_DIM, 1)
        tile0 = jnp.where(lane < HEAD_DIM, kt, onehot).astype(BF16)
        kaug_ref[hd, 0] = jnp.concatenate([tile0, tile1], axis=1)


def _l1_in(x, mod, g, w_in, tl, prompt):
    b, l, d = x.shape
    nt = l // tl
    row = pl.BlockSpec((b, tl, d), lambda i: (0, i, 0))
    out_specs = [row, row, row]
    out_shape = [jax.ShapeDtypeStruct((b, l, d), F32)] * 3
    if prompt:
        assert b == 1 and tl == MOBA_BLOCK and nt <= MAX_BLOCKS and nt % KV_GROUP == 0
        out_specs += [pl.BlockSpec((N_HEADS, HEAD_DIM, tl), lambda i: (0, 0, i)),
                      pl.BlockSpec((N_HEADS, 1, tl, 2 * LANES), lambda i: (0, i, 0, 0)),
                      pl.BlockSpec((N_HEADS, 1, V_ROWS, tl),
                                   lambda i: (0, i // KV_GROUP, 0, i % KV_GROUP)),
                      pl.BlockSpec((1, 1, d), lambda i: (i, 0, 0))]
        out_shape += [jax.ShapeDtypeStruct((N_HEADS, HEAD_DIM, l), F32),
                      jax.ShapeDtypeStruct((N_HEADS, nt, tl, 2 * LANES), BF16),
                      jax.ShapeDtypeStruct((N_HEADS, nt // KV_GROUP, V_ROWS, KV_GROUP * tl), BF16),
                      jax.ShapeDtypeStruct((nt, 1, d), F32)]
    else:
        out_specs += [row]
        out_shape += [jax.ShapeDtypeStruct((b, l, d), F32)]
    return pl.pallas_call(
        functools.partial(_l1_in_kernel, prompt),
        grid=(nt,),
        in_specs=[row,
                  pl.BlockSpec((b, 1, 3 * d), lambda i: (0, 0, 0)),
                  pl.BlockSpec((1, d), lambda i: (0, 0)),
                  pl.BlockSpec(w_in.shape, lambda i: (0, 0))],
        out_specs=out_specs,
        out_shape=out_shape,
        compiler_params=_cparams("arbitrary"),
    )(x, mod, g.reshape(1, d), w_in)


def _select_topk(gate, n_past):
    nb = gate.shape[0]
    jidx = lax.broadcasted_iota(jnp.int32, gate.shape, 0)
    jf = jidx.astype(F32)
    past = jidx < n_past
    gm = jnp.where(past, gate, -jnp.inf)
    sel = None
    for _ in range(MOBA_TOPK):
        m = jnp.max(gm, axis=0, keepdims=True)
        idx = jnp.min(jnp.where(gm == m, jf, float(nb)), axis=0, keepdims=True)
        pick = jf == idx
        sel = pick if sel is None else jnp.logical_or(sel, pick)
        gm = jnp.where(pick, -jnp.inf, gm)
    return jnp.logical_and(sel, past)


def _split_bf16(a):
    hi = a.astype(BF16)
    lo = (a - hi.astype(F32)).astype(BF16)
    return hi, lo


def _query_features(qt, kbar, slope, tile):
    tq = qt.shape[1]
    slope = slope * LOG2E
    gate = jnp.dot(kbar, qt, precision=HIGHEST, preferred_element_type=F32)
    jidx = lax.broadcasted_iota(jnp.int32, gate.shape, 0)
    lane = lax.broadcasted_iota(jnp.int32, gate.shape, 1)
    blk = tile * (tq // MOBA_BLOCK) + lane // MOBA_BLOCK
    sel = _select_topk(gate, blk)
    bias = -(slope * float(MOBA_BLOCK)) * (blk - jidx).astype(F32)
    add = jnp.where(sel, bias, jnp.where(jidx == blk, 0.0, NEG_INF))
    a_hi, a_lo = _split_bf16(add)
    s_hi, s_lo = _split_bf16(slope)
    tail = jnp.where(jidx == 0, s_hi.astype(F32), jnp.where(jidx == 1, s_lo.astype(F32), 0.0))
    qs = qt * (HEAD_DIM ** -0.5 * LOG2E)
    qa = jnp.concatenate([qs, a_hi.astype(F32), a_lo.astype(F32), tail], axis=0)
    return qa.astype(BF16)


def _attn_kernel(qt_ref, kbar_ref, slope_ref, kaug_ref, vt_ref, o_ref, s_a, s_b):
    tile = pl.program_id(1)
    tq = qt_ref.shape[2]
    gk = kaug_ref.shape[2]
    slope = slope_ref[0][0:1, 0:1]
    qa = _query_features(qt_ref[0], kbar_ref[0], slope, tile)
    last = lax.div(tile * tq, gk)

    def qk(g, dst):
        dst[...] = jnp.dot(kaug_ref[0, g], qa, preferred_element_type=F32)

    def update(src, g, carry, causal):
        m, acc = carry
        s = src[...]
        if causal:
            kpos = g * gk + lax.broadcasted_iota(jnp.int32, s.shape, 0)
            qpos = tile * tq + lax.broadcasted_iota(jnp.int32, s.shape, 1)
            s = jnp.where(kpos <= qpos, s, NEG_INF)
        m_new = jnp.maximum(m, jnp.max(s, axis=0, keepdims=True))
        p = jnp.exp2(s - m_new).astype(BF16)
        acc = jnp.exp2(m - m_new) * acc + jnp.dot(vt_ref[0, g], p, preferred_element_type=F32)
        return m_new, acc

    def pair(t, carry):
        g = 2 * t
        qk(g + 1, s_b)
        carry = update(s_a, g, carry, False)
        qk(g + 2, s_a)
        return update(s_b, g + 1, carry, False)

    def tail_even(carry):
        return update(s_a, last, carry, True)

    def tail_odd(carry):
        qk(last, s_b)
        carry = update(s_a, last - 1, carry, False)
        return update(s_b, last, carry, True)

    init = (jnp.full((1, tq), -jnp.inf, F32), jnp.zeros((vt_ref.shape[2], tq), F32))
    qk(0, s_a)
    carry = lax.fori_loop(0, last // 2, pair, init)
    _, acc = lax.cond(last % 2 == 0, tail_even, tail_odd, carry)
    o_ref[0] = acc[0:HEAD_DIM] / acc[HEAD_DIM:HEAD_DIM + 1]


def _attn(qt, kbar_h, slope_t, kaug_g, vt_g, tq):
    nh, hd, l = qt.shape
    ng, gk, f = kaug_g.shape[1:]
    assert gk % tq == 0 and l % tq == 0
    return pl.pallas_call(
        _attn_kernel,
        grid=(nh, l // tq),
        in_specs=[pl.BlockSpec((1, hd, tq), lambda h, i: (h, 0, i)),
                  pl.BlockSpec((1, MAX_BLOCKS, hd), lambda h, i: (h, 0, 0)),
                  pl.BlockSpec((1, SUBLANES, LANES), lambda h, i: (h, 0, 0)),
                  pl.BlockSpec((1, ng, gk, f), lambda h, i: (h, 0, 0, 0)),
                  pl.BlockSpec((1, ng, vt_g.shape[2], gk), lambda h, i: (h, 0, 0, 0))],
        out_specs=pl.BlockSpec((1, hd, tq), lambda h, i: (h, 0, i)),
        out_shape=jax.ShapeDtypeStruct((nh, hd, l), F32),
        scratch_shapes=[pltpu.VMEM((gk, tq), F32), pltpu.VMEM((gk, tq), F32)],
        compiler_params=_cparams("arbitrary", "arbitrary"),
    )(qt, kbar_h, slope_t, kaug_g, vt_g)


def _fold_heads(pv):
    rows, d = pv.shape
    r = lax.broadcasted_iota(jnp.int32, (rows, LANES), 0) // SUBLANES
    c = lax.broadcasted_iota(jnp.int32, (rows, LANES), 1) // HEAD_DIM
    out = jnp.zeros((rows, LANES), F32)
    for t in range(d // LANES):
        out = out + jnp.where(r == 2 * t + c, pv[:, LANES * t:LANES * (t + 1)], 0.0)
    return out


def _sample_attn_kernel(pt_ref, k0_ref, k1_ref, v0_ref, v1_ref, qs_ref, qf_ref, kn_ref, vn_ref,
                        slope_c_ref, qoff_c_ref, slope_r_ref, qoff_r_ref, o_ref,
                        g_s, m_s, l_s, o_s):
    j = pl.program_id(1)
    nb = g_s.shape[0]
    n, d = qs_ref.shape[1], qs_ref.shape[2]
    qs = qs_ref[0]
    slope_c = slope_c_ref[...]
    lane_n = lax.broadcasted_iota(jnp.int32, (n, LANES), 1)

    @pl.when(j == 0)
    def _():
        m_s[...] = jnp.zeros(m_s.shape, F32)
        l_s[...] = jnp.zeros(l_s.shape, F32)

    def put_column(ref, col, idx):
        ref[...] = jnp.where(lane_n == idx, col, ref[...])

    def partial(s):
        m = jnp.max(s, axis=1, keepdims=True)
        p = jnp.exp(s - m)
        return m, jnp.sum(p, axis=1, keepdims=True), p.astype(BF16)

    kt = jnp.concatenate([k0_ref[0, 0].reshape(d, PAGE_SIZE), k1_ref[0, 0].reshape(d, PAGE_SIZE)],
                         axis=1)
    vt = jnp.concatenate([v0_ref[0, 0].reshape(d, PAGE_SIZE), v1_ref[0, 0].reshape(d, PAGE_SIZE)],
                         axis=1)
    cpos = lax.broadcasted_iota(jnp.int32, (1, MOBA_BLOCK), 1).astype(F32)
    s = jnp.dot(qs, kt.astype(BF16), preferred_element_type=F32) + slope_c * cpos
    m, l, p = partial(s)
    put_column(m_s, m, j)
    put_column(l_s, l, j)
    pv = lax.dot_general(p, vt.astype(BF16), (((1,), (1,)), ((), ())),
                         preferred_element_type=F32)
    o_s[j] = _fold_heads(pv)
    ksum = jnp.sum(kt, axis=1, keepdims=True)
    g_s[pl.ds(j, 1), :] = jnp.sum(ksum * qf_ref[0], axis=0, keepdims=True) * (1.0 / MOBA_BLOCK)

    @pl.when(j == nb - 1)
    def _():
        nq = kn_ref.shape[1]
        pad = jnp.zeros((LANES - nq, d), F32)
        kn = jnp.concatenate([kn_ref[0], pad], axis=0).astype(BF16)
        vn = jnp.concatenate([vn_ref[0], pad], axis=0).astype(BF16)
        kpos = lax.broadcasted_iota(jnp.int32, (1, LANES), 1).astype(F32)
        qoff_c = qoff_c_ref[...]
        s_own = lax.dot_general(qs, kn, (((1,), (1,)), ((), ())), preferred_element_type=F32) \
            - slope_c * (qoff_c - kpos)
        s_own = jnp.where(kpos <= qoff_c, s_own, NEG_INF)
        m_own, l_own, p_own = partial(s_own)
        put_column(m_s, m_own, nb)
        put_column(l_s, l_own, nb)
        o_own = _fold_heads(jnp.dot(p_own, vn, preferred_element_type=F32))
        m_r = m_s[...].T
        l_r = l_s[...].T
        gate = jnp.concatenate([g_s[...], jnp.zeros((LANES - nb, n), F32)], axis=0)
        jidx = lax.broadcasted_iota(jnp.int32, (LANES, n), 0)
        sel = jnp.logical_or(_select_topk(gate, nb), jidx == nb)
        past_len = float(nb * MOBA_BLOCK)
        bias = slope_r_ref[...] * (past_len + qoff_r_ref[...] - (jidx * MOBA_BLOCK).astype(F32))
        mj = jnp.where(sel, m_r - jnp.where(jidx == nb, 0.0, bias), -jnp.inf)
        m_tot = jnp.max(mj, axis=0, keepdims=True)
        w = jnp.where(sel, jnp.exp(mj - m_tot), 0.0)
        l_tot = jnp.sum(w * l_r, axis=0, keepdims=True)
        w_t = (w / l_tot).T

        def column(jj):
            return jnp.sum(jnp.where(lane_n == jj, w_t, 0.0), axis=1, keepdims=True)

        acc = lax.fori_loop(0, nb, lambda jj, a: a + column(jj) * o_s[jj],
                            column(nb) * o_own)
        lane = lax.broadcasted_iota(jnp.int32, (nq, LANES), 1)
        tiles = []
        for t in range(N_HEADS // 2):
            ev = acc[SUBLANES * 2 * t:SUBLANES * (2 * t + 1), :]
            od = acc[SUBLANES * (2 * t + 1):SUBLANES * (2 * t + 2), :]
            tiles.append(jnp.where(lane < HEAD_DIM, ev, od))
        o_ref[0] = jnp.concatenate(tiles, axis=1)


def _sample_attn(page_table, cache_kt, cache_vt, layer, qs_rows, qf_cols, k_new, v_new, slopes):
    b, n_pages = page_table.shape
    ppb = MOBA_BLOCK // PAGE_SIZE
    nb = n_pages // ppb
    assert n_pages % ppb == 0 and ppb == 2 and nb < LANES and nb % SUBLANES == 0
    lq, d = k_new.shape[1], k_new.shape[2]
    n = qs_rows.shape[1]
    assert lq == SUBLANES and n == LANES
    slope_row = jnp.repeat(slopes, lq)[None, :]
    qoff_row = jnp.tile(jnp.arange(lq, dtype=F32), N_HEADS)[None, :]
    page = lambda off: pl.BlockSpec(
        (1, 1, N_HEADS, HEAD_DIM, PAGE_SIZE),
        lambda s, j, pt: (layer, pt[s, ppb * j + off], 0, 0, 0))
    per_seq = lambda r, c: pl.BlockSpec((1, r, c), lambda s, j, pt: (s, 0, 0))
    col = pl.BlockSpec((n, 1), lambda s, j, pt: (0, 0))
    row = pl.BlockSpec((1, n), lambda s, j, pt: (0, 0))
    grid_spec = pltpu.PrefetchScalarGridSpec(
        num_scalar_prefetch=1,
        grid=(b, nb),
        in_specs=[page(0), page(1), page(0), page(1),
                  per_seq(n, d), per_seq(d, n), per_seq(lq, d), per_seq(lq, d),
                  col, col, row, row],
        out_specs=per_seq(lq, d),
        scratch_shapes=[pltpu.VMEM((nb, n), F32),
                        pltpu.VMEM((n, LANES), F32),
                        pltpu.VMEM((n, LANES), F32),
                        pltpu.VMEM((nb, n, LANES), F32)],
    )
    return pl.pallas_call(
        _sample_attn_kernel,
        grid_spec=grid_spec,
        out_shape=jax.ShapeDtypeStruct((b, lq, d), F32),
        compiler_params=_cparams("arbitrary", "arbitrary"),
    )(page_table, cache_kt, cache_kt, cache_vt, cache_vt, qs_rows, qf_cols, k_new, v_new,
      slope_row.T, qoff_row.T, slope_row, qoff_row)


def _l1_out_kernel(transposed, x_ref, o_ref, sz_ref, mod_ref, g_ref, w_ref, y_ref):
    ns, tl, d = x_ref.shape
    rows = ns * tl
    if transposed:
        o = o_ref[...].reshape(d, rows).T
    else:
        o = o_ref[...].reshape(rows, d)
    a = (o * sz_ref[...].reshape(rows, d)).astype(BF16)
    out = jnp.dot(a, w_ref[...], preferred_element_type=F32).reshape(ns, tl, d)
    gate = mod_ref[...][:, :, 2 * d:3 * d]
    y_ref[...] = x_ref[...] + (1.0 + gate) * _rms(out, g_ref[...])


def _l1_out(x, o, sz, mod, g, w_out, tl, transposed):
    b, l, d = x.shape
    row = pl.BlockSpec((b, tl, d), lambda i: (0, i, 0))
    if transposed:
        o_spec = pl.BlockSpec((N_HEADS, HEAD_DIM, tl), lambda i: (0, 0, i))
    else:
        o_spec = row
    return pl.pallas_call(
        functools.partial(_l1_out_kernel, transposed),
        grid=(l // tl,),
        in_specs=[row, o_spec, row,
                  pl.BlockSpec((b, 1, 3 * d), lambda i: (0, 0, 0)),
                  pl.BlockSpec((1, d), lambda i: (0, 0)),
                  pl.BlockSpec(w_out.shape, lambda i: (0, 0))],
        out_specs=row,
        out_shape=jax.ShapeDtypeStruct((b, l, d), F32),
        compiler_params=_cparams("arbitrary"),
    )(x, o, sz, mod, g.reshape(1, d), w_out)


def _s5_weights(lbr, lbi, bbr, bbi, c_re, c_im):
    g = c_re.shape[0]
    hh, p = SSM_GROUP, SSM_STATE
    oct_n = g // 8
    lam_r = lbr.reshape(g, hh, p)[:, 0, :].reshape(oct_n, 8 * p)
    lam_i = lbi.reshape(g, hh, p)[:, 0, :].reshape(oct_n, 8 * p)
    lre = jnp.concatenate([lam_r, lam_r], axis=0)
    lim = jnp.concatenate([-lam_i, lam_i], axis=0)
    eye = jnp.eye(8, dtype=F32)

    def in_w(bb):
        bb = bb.reshape(oct_n, 8, hh, p)
        return jnp.einsum('oghp,gk->oghkp', bb, eye).reshape(oct_n, 8 * hh, 8 * p)

    def out_w(cc):
        cc = cc.reshape(oct_n, 8, hh, p)
        return jnp.einsum('oghp,gk->ogpkh', cc, eye).reshape(oct_n, 8 * p, 8 * hh)

    wb = jnp.concatenate([in_w(bbr), in_w(bbi)], axis=0).astype(BF16)
    wc = jnp.concatenate([out_w(c_re), -out_w(c_im)], axis=1).astype(BF16)
    return lre, lim, wb, wc


def _state_tiles(re, im):
    b = re.shape[0]
    return jnp.concatenate([re.reshape(b, 4, -1), im.reshape(b, 4, -1)], axis=1)


def _block_diag_q(q):
    b, lq, _ = q.shape
    qh = q.reshape(b, lq, N_HEADS, HEAD_DIM)
    eye = jnp.eye(N_HEADS, dtype=q.dtype)
    return jnp.einsum('bqhd,hk->bhdkq', qh, eye).reshape(b, N_HEADS * HEAD_DIM, N_HEADS * lq)


def kernel(x_prompt, x_sample, state_conv, state_ssm_re, state_ssm_im, cache_k, cache_v, page_table,
           c_prompt, c_sample, norm_pre, norm_post, ada_w, ada_b, w_in_even, conv_w, conv_b,
           ssm_lambda_re, ssm_lambda_im, ssm_log_dt, ssm_b_re, ssm_b_im, ssm_c_re, ssm_c_im,
           ssm_d, ssm_glu_w, w_out_even, w_in_odd, w_out_odd):
    bp, lp, d = x_prompt.shape
    bs, ls, _ = x_sample.shape
    g, p = ssm_lambda_re.shape[1], ssm_lambda_re.shape[2]
    n_pool = cache_k.shape[1]
    assert bp == 1 and ls == SUBLANES and d == N_HEADS * HEAD_DIM and g == 32 and p == SSM_STATE

    n_c = bp + bs
    c_all = jnp.concatenate(
        [c_prompt, c_sample, jnp.zeros((-n_c % SUBLANES, d), F32)], axis=0)
    mod = _adaln(c_all, ada_w, ada_b)
    mod_p = [mod[l, 0:bp][:, None, :] for l in range(2)]
    mod_s = [mod[l, bp:n_c][:, None, :] for l in range(2)]

    lbr, lbi, bbr, bbi = _s5_params(ssm_lambda_re[0], ssm_lambda_im[0], ssm_log_dt[0],
                                    ssm_b_re[0], ssm_b_im[0])
    lre, lim, wb, wc = _s5_weights(lbr, lbi, bbr, bbi, ssm_c_re[0], ssm_c_im[0])
    w_in0 = w_in_even[0].astype(BF16)
    glu_w = ssm_glu_w[0].astype(BF16)
    w_out0 = w_out_even[0].astype(BF16)
    wcv = conv_w.shape[2]

    def layer0(x, mods, conv0, h0_tiles, tl, tt):
        ya, u, sz, cbuf = _l0_in(x, mods, norm_pre[0], w_in0, conv_w[0], conv_b[0], conv0, tl)
        ys, h_last = _s5(u, h0_tiles, lre, lim, wb, wc, ssm_d[0], tt)
        x1 = _l0_out(x, ya, ys, sz, mods, norm_post[0], glu_w, w_out0, tl)
        b = x.shape[0]
        h_re = h_last[:, 0:4].reshape(b, g, p)
        h_im = h_last[:, 4:8].reshape(b, g, p)
        return x1, cbuf, h_re, h_im

    x1_p, conv_p, hre_p, him_p = layer0(
        x_prompt, mod_p[0], jnp.zeros((bp, 2, wcv), F32),
        jnp.zeros((bp, SUBLANES, 2 * g * p // SUBLANES), F32), 512, 256)
    x1_s, conv_s, hre_s, him_s = layer0(
        x_sample, mod_s[0], state_conv[0],
        _state_tiles(state_ssm_re[0], state_ssm_im[0]), ls, ls)

    w_in1 = w_in_odd[0].astype(BF16)
    w_out1 = w_out_odd[0].astype(BF16)
    slopes = jnp.exp2(-8.0 * jnp.arange(1, N_HEADS + 1, dtype=F32) / N_HEADS)

    k_p, v_p, sz_p, qt, kaug, vt, kbar = _l1_in(x1_p, mod_p[1], norm_pre[1], w_in1, MOBA_BLOCK, True)
    nt = lp // MOBA_BLOCK
    kbar_h = kbar.reshape(nt, N_HEADS, HEAD_DIM).transpose(1, 0, 2)
    kbar_h = jnp.pad(kbar_h, ((0, 0), (0, MAX_BLOCKS - nt), (0, 0)))
    slope_t = jnp.broadcast_to(slopes[:, None, None], (N_HEADS, SUBLANES, LANES))
    kaug_g = kaug.reshape(N_HEADS, nt // KV_GROUP, KV_GROUP * MOBA_BLOCK, kaug.shape[3])
    o_t = _attn(qt, kbar_h, slope_t, kaug_g, vt, min(Q_TILE, KV_GROUP * MOBA_BLOCK))
    y_p = _l1_out(x1_p, o_t, sz_p, mod_p[1], norm_post[1], w_out1, MOBA_BLOCK, True)

    k_s, v_s, sz_s, q_s = _l1_in(x1_s, mod_s[1], norm_pre[1], w_in1, ls, False)
    qf_cols = _block_diag_q(q_s)
    qs_rows = (qf_cols * (HEAD_DIM ** -0.5)).astype(BF16).transpose(0, 2, 1)
    ckt = cache_k.transpose(0, 1, 3, 4, 2)
    cvt = cache_v.transpose(0, 1, 3, 4, 2)
    o_s = _sample_attn(page_table, ckt, cvt, 0, qs_rows, qf_cols, k_s, v_s, slopes)
    y_s = _l1_out(x1_s, o_s, sz_s, mod_s[1], norm_post[1], w_out1, ls, False)

    heads = lambda t: t.reshape(1, t.shape[0], t.shape[1], N_HEADS, HEAD_DIM)
    return (y_p, y_s, conv_p[None], conv_s[None],
            hre_p[None], him_p[None], hre_s[None], him_s[None],
            heads(k_p), heads(v_p), heads(k_s), heads(v_s))
```

```python
import functools

import jax
import jax.numpy as jnp
from jax import lax
from jax.experimental import pallas as pl
from jax.experimental.pallas import tpu as pltpu

F32 = jnp.float32
BF16 = jnp.bfloat16
HIGHEST = lax.Precision.HIGHEST

EPS = 1e-6
NEG_INF = -1e30
N_HEADS = 16
HEAD_DIM = 64
MOBA_BLOCK = 256
MOBA_TOPK = 3
PAGE_SIZE = 128
SSM_GROUP = 16
SSM_STATE = 64
MAX_BLOCKS = 64
KV_GROUP = 4
Q_TILE = 512
V_ROWS = 80
SAMPLE_BLOCKS_PER_STEP = 4
LOG2E = 1.4426950408889634
LANES = 128
SUBLANES = 8
VMEM_LIMIT = 48 * 1024 * 1024


def _cparams(*sem):
    return pltpu.CompilerParams(dimension_semantics=sem, vmem_limit_bytes=VMEM_LIMIT)


def _silu(x):
    return x * (1.0 / (1.0 + jnp.exp(-x)))


def _sigmoid(x):
    return 1.0 / (1.0 + jnp.exp(-x))


def _gelu_tanh(x):
    c = 0.7978845608028654
    return 0.5 * x * (1.0 + jnp.tanh(c * (x + 0.044715 * (x * x * x))))


def _rms(x, g):
    ms = jnp.mean(x * x, axis=-1, keepdims=True)
    return x * lax.rsqrt(ms + EPS) * g


def _adaln_kernel(c_ref, w_ref, b_ref, o_ref):
    c = c_ref[...]
    o_ref[0] = jnp.dot(_silu(c), w_ref[0], precision=HIGHEST,
                       preferred_element_type=F32) + b_ref[0]


def _adaln(c_all, ada_w, ada_b):
    depth, d, d3 = ada_w.shape
    r = c_all.shape[0]
    tn = 1024
    return pl.pallas_call(
        _adaln_kernel,
        grid=(depth, d3 // tn),
        in_specs=[pl.BlockSpec((r, d), lambda l, n: (0, 0)),
                  pl.BlockSpec((1, d, tn), lambda l, n: (l, 0, n)),
                  pl.BlockSpec((1, 1, tn), lambda l, n: (l, 0, n))],
        out_specs=pl.BlockSpec((1, r, tn), lambda l, n: (l, 0, n)),
        out_shape=jax.ShapeDtypeStruct((depth, r, d3), F32),
        compiler_params=_cparams("arbitrary", "arbitrary"),
    )(c_all, ada_w, ada_b.reshape(depth, 1, d3))


def _l0_in_kernel(x_ref, mod_ref, g_ref, w_ref, cw_ref, cb_ref, c0_ref,
                  ya_ref, u_ref, sz_ref, cbuf_ref, fbuf):
    ns, tl, d = x_ref.shape
    w = cw_ref.shape[1]

    @pl.when(pl.program_id(0) == 0)
    def _():
        fbuf[:, 6:8, :] = c0_ref[...]

    mod = mod_ref[...]
    shift, scale = mod[:, :, 0:d], mod[:, :, d:2 * d]
    h = _rms(x_ref[...], g_ref[...]) * (1.0 + scale) + shift
    proj = jnp.dot(h.reshape(ns * tl, d).astype(BF16), w_ref[...],
                   preferred_element_type=F32)
    xa, ba, ca = proj[:, 0:w], proj[:, w:2 * w], proj[:, 2 * w:3 * w]
    za, us, zs = proj[:, 3 * w:4 * w], proj[:, 4 * w:5 * w], proj[:, 5 * w:6 * w]
    f = (ca * xa).reshape(ns, tl, w)
    fbuf[:, 8:8 + tl, :] = f
    f1 = fbuf[:, 7:7 + tl, :]
    f2 = fbuf[:, 6:6 + tl, :]
    cw = cw_ref[...]
    conv = cb_ref[...] + cw[0:1] * f2 + cw[1:2] * f1 + cw[2:3] * f
    ya = ba.reshape(ns, tl, w) * conv * _silu(za).reshape(ns, tl, w)
    ya_ref[...] = ya
    u_ref[...] = us.reshape(ns, tl, w)
    sz_ref[...] = _silu(zs).reshape(ns, tl, w)
    tail = fbuf[:, tl + 6:tl + 8, :]
    cbuf_ref[...] = tail
    fbuf[:, 6:8, :] = tail


def _l0_in(x, mod, g, w_in, conv_w, conv_b, conv0, tl):
    b, l, d = x.shape
    w = conv_w.shape[1]
    ns = b
    return pl.pallas_call(
        _l0_in_kernel,
        grid=(l // tl,),
        in_specs=[pl.BlockSpec((ns, tl, d), lambda i: (0, i, 0)),
                  pl.BlockSpec((ns, 1, 3 * d), lambda i: (0, 0, 0)),
                  pl.BlockSpec((1, d), lambda i: (0, 0)),
                  pl.BlockSpec(w_in.shape, lambda i: (0, 0)),
                  pl.BlockSpec(conv_w.shape, lambda i: (0, 0)),
                  pl.BlockSpec((1, w), lambda i: (0, 0)),
                  pl.BlockSpec((ns, 2, w), lambda i: (0, 0, 0))],
        out_specs=[pl.BlockSpec((ns, tl, w), lambda i: (0, i, 0)),
                   pl.BlockSpec((ns, tl, w), lambda i: (0, i, 0)),
                   pl.BlockSpec((ns, tl, w), lambda i: (0, i, 0)),
                   pl.BlockSpec((ns, 2, w), lambda i: (0, 0, 0))],
        out_shape=[jax.ShapeDtypeStruct((b, l, w), F32),
                   jax.ShapeDtypeStruct((b, l, w), F32),
                   jax.ShapeDtypeStruct((b, l, w), F32),
                   jax.ShapeDtypeStruct((b, 2, w), F32)],
        scratch_shapes=[pltpu.VMEM((ns, tl + 8, w), F32)],
        compiler_params=_cparams("arbitrary"),
    )(x, mod, g.reshape(1, d), w_in, conv_w, conv_b.reshape(1, w), conv0)


def _s5_param_kernel(lr_ref, li_ref, ldt_ref, br_ref, bi_ref,
                     lbr_ref, lbi_ref, bbr_ref, bbi_ref):
    lr, li = lr_ref[...], li_ref[...]
    dt = jnp.exp(ldt_ref[...])
    mag = jnp.exp(lr * dt)
    ang = li * dt
    lbr = mag * jnp.cos(ang)
    lbi = mag * jnp.sin(ang)
    nr, ni = lbr - 1.0, lbi
    den = lr * lr + li * li
    cr = (nr * lr + ni * li) / den
    ci = (ni * lr - nr * li) / den
    br, bi = br_ref[...], bi_ref[...]
    lbr_ref[...] = lbr
    lbi_ref[...] = lbi
    bbr_ref[...] = cr * br - ci * bi
    bbi_ref[...] = cr * bi + ci * br


def _s5_params(lam_re, lam_im, log_dt, b_re, b_im):
    g, p = lam_re.shape
    hh = b_re.shape[2]
    rows = g * hh
    rep = lambda a: jnp.repeat(a, hh, axis=0)
    ldt = jnp.broadcast_to(log_dt[:, None], (g, p))
    br = b_re.transpose(0, 2, 1).reshape(rows, p)
    bi = b_im.transpose(0, 2, 1).reshape(rows, p)
    spec = pl.BlockSpec((rows, p), lambda: (0, 0))
    return pl.pallas_call(
        _s5_param_kernel,
        in_specs=[spec] * 5,
        out_specs=[spec] * 4,
        out_shape=[jax.ShapeDtypeStruct((rows, p), F32)] * 4,
    )(rep(lam_re), rep(lam_im), rep(ldt), br, bi)


def _s5_kernel(u_ref, h0_ref, lre_ref, lim_ref, wb_ref, wc_ref, d_ref,
               y_ref, hl_ref, hs, scr):
    tt = u_ref.shape[1]
    oct_w = wb_ref.shape[1]

    @pl.when(pl.program_id(1) == 0)
    def _():
        hs[...] = h0_ref[0]

    u = u_ref[0]
    ub = u.astype(BF16)
    nlt = scr.shape[0]
    for j in range(SUBLANES):
        o = j % 4
        bu = jnp.dot(ub[:, oct_w * o:oct_w * (o + 1)], wb_ref[j], preferred_element_type=F32)
        for c in range(nlt):
            scr[c, pl.ds(j, tt, stride=SUBLANES), :] = bu[:, LANES * c:LANES * (c + 1)]
    lre = [lre_ref[:, LANES * c:LANES * (c + 1)] for c in range(nlt)]
    lim = [lim_ref[:, LANES * c:LANES * (c + 1)] for c in range(nlt)]

    def body(i, hc):
        r = pl.multiple_of(i * SUBLANES, SUBLANES)
        out = []
        for c in range(nlt):
            h = lre[c] * hc[c] + lim[c] * pltpu.roll(hc[c], 4, 0) + scr[c, pl.ds(r, SUBLANES), :]
            scr[c, pl.ds(r, SUBLANES), :] = h
            out.append(h)
        return tuple(out)

    h0 = tuple(hs[:, LANES * c:LANES * (c + 1)] for c in range(nlt))
    hc = lax.fori_loop(0, tt, body, h0, unroll=8)
    h = jnp.concatenate(hc, axis=1)
    hs[...] = h
    hl_ref[0] = h
    ys = []
    for o in range(4):
        parts = [scr[c, pl.ds(part + o, tt, stride=SUBLANES), :]
                 for part in (0, 4) for c in range(nlt)]
        hcat = jnp.concatenate(parts, axis=1).astype(BF16)
        ys.append(jnp.dot(hcat, wc_ref[o], preferred_element_type=F32))
    y_ref[0] = jnp.concatenate(ys, axis=1) + d_ref[...] * u


def _s5(u, h0_tiles, lre, lim, wb, wc, d, tt):
    b, l, w = u.shape
    sw = lre.shape[1]
    return pl.pallas_call(
        _s5_kernel,
        grid=(b, l // tt),
        in_specs=[pl.BlockSpec((1, tt, w), lambda s, t: (s, t, 0)),
                  pl.BlockSpec((1, SUBLANES, sw), lambda s, t: (s, 0, 0)),
                  pl.BlockSpec(lre.shape, lambda s, t: (0, 0)),
                  pl.BlockSpec(lim.shape, lambda s, t: (0, 0)),
                  pl.BlockSpec(wb.shape, lambda s, t: (0, 0, 0)),
                  pl.BlockSpec(wc.shape, lambda s, t: (0, 0, 0)),
                  pl.BlockSpec((1, w), lambda s, t: (0, 0))],
        out_specs=[pl.BlockSpec((1, tt, w), lambda s, t: (s, t, 0)),
                   pl.BlockSpec((1, SUBLANES, sw), lambda s, t: (s, 0, 0))],
        out_shape=[jax.ShapeDtypeStruct((b, l, w), F32),
                   jax.ShapeDtypeStruct((b, SUBLANES, sw), F32)],
        scratch_shapes=[pltpu.VMEM((SUBLANES, sw), F32),
                        pltpu.VMEM((sw // LANES, tt * SUBLANES, LANES), F32)],
        compiler_params=_cparams("arbitrary", "arbitrary"),
    )(u, h0_tiles, lre, lim, wb, wc, d.reshape(1, w))


def _l0_out_kernel(x_ref, ya_ref, ys_ref, sz_ref, mod_ref, g_ref, glu_ref, w_ref, o_ref):
    ns, tl, d = x_ref.shape
    w = ya_ref.shape[2]
    rows = ns * tl
    g1 = _gelu_tanh(ys_ref[...].reshape(rows, w))
    lin = jnp.dot(g1.astype(BF16), glu_ref[...], preferred_element_type=F32)
    ys = g1 * _sigmoid(lin) * sz_ref[...].reshape(rows, w)
    cat = jnp.concatenate([ya_ref[...].reshape(rows, w).astype(BF16), ys.astype(BF16)], axis=1)
    out = jnp.dot(cat, w_ref[...], preferred_element_type=F32).reshape(ns, tl, d)
    gate = mod_ref[...][:, :, 2 * d:3 * d]
    o_ref[...] = x_ref[...] + (1.0 + gate) * _rms(out, g_ref[...])


def _l0_out(x, ya, ys, sz, mod, g, glu_w, w_out, tl):
    b, l, d = x.shape
    w = ya.shape[2]
    row = lambda width: pl.BlockSpec((b, tl, width), lambda i: (0, i, 0))
    return pl.pallas_call(
        _l0_out_kernel,
        grid=(l // tl,),
        in_specs=[row(d), row(w), row(w), row(w),
                  pl.BlockSpec((b, 1, 3 * d), lambda i: (0, 0, 0)),
                  pl.BlockSpec((1, d), lambda i: (0, 0)),
                  pl.BlockSpec(glu_w.shape, lambda i: (0, 0)),
                  pl.BlockSpec(w_out.shape, lambda i: (0, 0))],
        out_specs=row(d),
        out_shape=jax.ShapeDtypeStruct((b, l, d), F32),
        compiler_params=_cparams("arbitrary"),
    )(x, ya, ys, sz, mod, g.reshape(1, d), glu_w, w_out)


def _l1_in_kernel(prompt, x_ref, mod_ref, g_ref, w_ref, k_ref, v_ref, sz_ref, *rest):
    ns, tl, d = x_ref.shape
    rows = ns * tl
    mod = mod_ref[...]
    shift, scale = mod[:, :, 0:d], mod[:, :, d:2 * d]
    h = _rms(x_ref[...], g_ref[...]) * (1.0 + scale) + shift
    proj = jnp.dot(h.reshape(rows, d).astype(BF16), w_ref[...], preferred_element_type=F32)
    q, k, v, z = (proj[:, i * d:(i + 1) * d] for i in range(4))
    k_ref[...] = k.reshape(ns, tl, d)
    v_ref[...] = v.reshape(ns, tl, d)
    sz_ref[...] = _silu(z).reshape(ns, tl, d)
    if not prompt:
        (q_ref,) = rest
        q_ref[...] = q.reshape(ns, tl, d)
        return
    qt_ref, kaug_ref, vt_ref, kbar_ref = rest
    i = pl.program_id(0)
    qt_ref[...] = q.T.reshape(N_HEADS, HEAD_DIM, rows)
    ones_rows = (lax.broadcasted_iota(jnp.int32, (N_HEADS, V_ROWS - HEAD_DIM, rows), 1) == 0)
    vt = jnp.concatenate([v.T.reshape(N_HEADS, HEAD_DIM, rows), ones_rows.astype(F32)], axis=1)
    vt_ref[...] = vt.reshape(N_HEADS, 1, V_ROWS, rows).astype(BF16)
    kbar_ref[0] = jnp.mean(k, axis=0, keepdims=True)
    lane = lax.broadcasted_iota(jnp.int32, (rows, LANES), 1)
    rowi = lax.broadcasted_iota(jnp.int32, (rows, LANES), 0)
    onehot = (lane - HEAD_DIM == i).astype(F32)
    tile1 = jnp.where(lane < MAX_BLOCKS, (lane == i).astype(F32),
                      jnp.where(lane < MAX_BLOCKS + 2, rowi.astype(F32), 0.0)).astype(BF16)
    for hd in range(N_HEADS):
        kt = k[:, LANES * (hd // 2):LANES * (hd // 2 + 1)]
        if hd % 2:
            kt = pltpu.roll(kt, HEAD_DIM, 1)
        tile0 = jnp.where(lane < HEAD_DIM, kt, onehot).astype(BF16)
        kaug_ref[hd, 0] = jnp.concatenate([tile0, tile1], axis=1)


def _l1_in(x, mod, g, w_in, tl, prompt):
    b, l, d = x.shape
    nt = l // tl
    row = pl.BlockSpec((b, tl, d), lambda i: (0, i, 0))
    out_specs = [row, row, row]
    out_shape = [jax.ShapeDtypeStruct((b, l, d), F32)] * 3
    if prompt:
        assert b == 1 and tl == MOBA_BLOCK and nt <= MAX_BLOCKS and nt % KV_GROUP == 0
        out_specs += [pl.BlockSpec((N_HEADS, HEAD_DIM, tl), lambda i: (0, 0, i)),
                      pl.BlockSpec((N_HEADS, 1, tl, 2 * LANES), lambda i: (0, i, 0, 0)),
                      pl.BlockSpec((N_HEADS, 1, V_ROWS, tl),
                                   lambda i: (0, i // KV_GROUP, 0, i % KV_GROUP)),
                      pl.BlockSpec((1, 1, d), lambda i: (i, 0, 0))]
        out_shape += [jax.ShapeDtypeStruct((N_HEADS, HEAD_DIM, l), F32),
                      jax.ShapeDtypeStruct((N_HEADS, nt, tl, 2 * LANES), BF16),
                      jax.ShapeDtypeStruct((N_HEADS, nt // KV_GROUP, V_ROWS, KV_GROUP * tl), BF16),
                      jax.ShapeDtypeStruct((nt, 1, d), F32)]
    else:
        out_specs += [row]
        out_shape += [jax.ShapeDtypeStruct((b, l, d), F32)]
    return pl.pallas_call(
        functools.partial(_l1_in_kernel, prompt),
        grid=(nt,),
        in_specs=[row,
                  pl.BlockSpec((b, 1, 3 * d), lambda i: (0, 0, 0)),
                  pl.BlockSpec((1, d), lambda i: (0, 0)),
                  pl.BlockSpec(w_in.shape, lambda i: (0, 0))],
        out_specs=out_specs,
        out_shape=out_shape,
        compiler_params=_cparams("arbitrary"),
    )(x, mod, g.reshape(1, d), w_in)


def _select_topk(gate, n_past):
    nb = gate.shape[0]
    jidx = lax.broadcasted_iota(jnp.int32, gate.shape, 0)
    jf = jidx.astype(F32)
    past = jidx < n_past
    gm = jnp.where(past, gate, -jnp.inf)
    sel = None
    for _ in range(MOBA_TOPK):
        m = jnp.max(gm, axis=0, keepdims=True)
        idx = jnp.min(jnp.where(gm == m, jf, float(nb)), axis=0, keepdims=True)
        pick = jf == idx
        sel = pick if sel is None else jnp.logical_or(sel, pick)
        gm = jnp.where(pick, -jnp.inf, gm)
    return jnp.logical_and(sel, past)


def _split_bf16(a):
    hi = a.astype(BF16)
    lo = (a - hi.astype(F32)).astype(BF16)
    return hi, lo


def _query_features(qt, kbar, slope, tile):
    tq = qt.shape[1]
    slope = slope * LOG2E
    gate = jnp.dot(kbar, qt, precision=HIGHEST, preferred_element_type=F32)
    jidx = lax.broadcasted_iota(jnp.int32, gate.shape, 0)
    lane = lax.broadcasted_iota(jnp.int32, gate.shape, 1)
    blk = tile * (tq // MOBA_BLOCK) + lane // MOBA_BLOCK
    sel = _select_topk(gate, blk)
    bias = -(slope * float(MOBA_BLOCK)) * (blk - jidx).astype(F32)
    add = jnp.where(sel, bias, jnp.where(jidx == blk, 0.0, NEG_INF))
    a_hi, a_lo = _split_bf16(add)
    s_hi, s_lo = _split_bf16(slope)
    tail = jnp.where(jidx == 0, s_hi.astype(F32), jnp.where(jidx == 1, s_lo.astype(F32), 0.0))
    qs = qt * (HEAD_DIM ** -0.5 * LOG2E)
    qa = jnp.concatenate([qs, a_hi.astype(F32), a_lo.astype(F32), tail], axis=0)
    return qa.astype(BF16)


def _attn_kernel(qt_ref, kbar_ref, slope_ref, kaug_ref, vt_ref, o_ref, s_a, s_b):
    tile = pl.program_id(1)
    tq = qt_ref.shape[2]
    gk = kaug_ref.shape[2]
    slope = slope_ref[0][0:1, 0:1]
    qa = _query_features(qt_ref[0], kbar_ref[0], slope, tile)
    last = lax.div(tile * tq, gk)

    def qk(g, dst):
        dst[...] = jnp.dot(kaug_ref[0, g], qa, preferred_element_type=F32)

    def update(src, g, carry, causal):
        m, acc = carry
        s = src[...]
        if causal:
            kpos = g * gk + lax.broadcasted_iota(jnp.int32, s.shape, 0)
            qpos = tile * tq + lax.broadcasted_iota(jnp.int32, s.shape, 1)
            s = jnp.where(kpos <= qpos, s, NEG_INF)
        m_new = jnp.maximum(m, jnp.max(s, axis=0, keepdims=True))
        p = jnp.exp2(s - m_new).astype(BF16)
        acc = jnp.exp2(m - m_new) * acc + jnp.dot(vt_ref[0, g], p, preferred_element_type=F32)
        return m_new, acc

    def pair(t, carry):
        g = 2 * t
        qk(g + 1, s_b)
        carry = update(s_a, g, carry, False)
        qk(g + 2, s_a)
        return update(s_b, g + 1, carry, False)

    def tail_even(carry):
        return update(s_a, last, carry, True)

    def tail_odd(carry):
        qk(last, s_b)
        carry = update(s_a, last - 1, carry, False)
        return update(s_b, last, carry, True)

    init = (jnp.full((1, tq), -jnp.inf, F32), jnp.zeros((vt_ref.shape[2], tq), F32))
    qk(0, s_a)
    carry = lax.fori_loop(0, last // 2, pair, init)
    _, acc = lax.cond(last % 2 == 0, tail_even, tail_odd, carry)
    o_ref[0] = acc[0:HEAD_DIM] / acc[HEAD_DIM:HEAD_DIM + 1]


def _attn(qt, kbar_h, slope_t, kaug_g, vt_g, tq):
    nh, hd, l = qt.shape
    ng, gk, f = kaug_g.shape[1:]
    assert gk % tq == 0 and l % tq == 0
    return pl.pallas_call(
        _attn_kernel,
        grid=(nh, l // tq),
        in_specs=[pl.BlockSpec((1, hd, tq), lambda h, i: (h, 0, i)),
                  pl.BlockSpec((1, MAX_BLOCKS, hd), lambda h, i: (h, 0, 0)),
                  pl.BlockSpec((1, SUBLANES, LANES), lambda h, i: (h, 0, 0)),
                  pl.BlockSpec((1, ng, gk, f), lambda h, i: (h, 0, 0, 0)),
                  pl.BlockSpec((1, ng, vt_g.shape[2], gk), lambda h, i: (h, 0, 0, 0))],
        out_specs=pl.BlockSpec((1, hd, tq), lambda h, i: (h, 0, i)),
        out_shape=jax.ShapeDtypeStruct((nh, hd, l), F32),
        scratch_shapes=[pltpu.VMEM((gk, tq), F32), pltpu.VMEM((gk, tq), F32)],
        compiler_params=_cparams("arbitrary", "arbitrary"),
    )(qt, kbar_h, slope_t, kaug_g, vt_g)


def _fold_heads(pv):
    rows, d = pv.shape
    r = lax.broadcasted_iota(jnp.int32, (rows, LANES), 0) // SUBLANES
    c = lax.broadcasted_iota(jnp.int32, (rows, LANES), 1) // HEAD_DIM
    out = jnp.zeros((rows, LANES), F32)
    for t in range(d // LANES):
        out = out + jnp.where(r == 2 * t + c, pv[:, LANES * t:LANES * (t + 1)], 0.0)
    return out


def _sample_attn_kernel(bps, pt_ref, *refs):
    ppb = MOBA_BLOCK // PAGE_SIZE
    npg = bps * ppb
    k_refs, v_refs = refs[0:npg], refs[npg:2 * npg]
    (q_ref, kn_ref, vn_ref, slope_c_ref, qoff_c_ref, slope_r_ref, qoff_r_ref,
     o_ref, g_s, m_s, l_s, o_s, qs_s, qf_s) = refs[2 * npg:]
    step = pl.program_id(1)
    nb = g_s.shape[0]
    n, d = qs_s.shape
    slope_c = slope_c_ref[...]
    lane_n = lax.broadcasted_iota(jnp.int32, (n, LANES), 1)

    @pl.when(step == 0)
    def _():
        m_s[...] = jnp.zeros(m_s.shape, F32)
        l_s[...] = jnp.zeros(l_s.shape, F32)
        q_rep = jnp.concatenate([q_ref[0]] * (n // q_ref.shape[1]), axis=0)
        row_h = lax.broadcasted_iota(jnp.int32, (n, d), 0) // q_ref.shape[1]
        col_h = lax.broadcasted_iota(jnp.int32, (n, d), 1) // HEAD_DIM
        q_bd = jnp.where(row_h == col_h, q_rep, 0.0)
        qs_s[...] = (q_bd * (HEAD_DIM ** -0.5)).astype(BF16)
        qf_s[...] = q_bd.T

    qs = qs_s[...]

    def put_column(ref, col, idx):
        ref[...] = jnp.where(lane_n == idx, col, ref[...])

    def partial(s):
        m = jnp.max(s, axis=1, keepdims=True)
        p = jnp.exp(s - m)
        return m, jnp.sum(p, axis=1, keepdims=True), p.astype(BF16)

    def pages(page_refs, blk):
        return jnp.concatenate([page_refs[ppb * blk + o][0, 0].reshape(d, PAGE_SIZE)
                                for o in range(ppb)], axis=1)

    cpos = lax.broadcasted_iota(jnp.int32, (1, MOBA_BLOCK), 1).astype(F32)
    m_all, l_all = m_s[...], l_s[...]
    for blk in range(bps):
        j = step * bps + blk
        kt = pages(k_refs, blk)
        s = jnp.dot(qs, kt.astype(BF16), preferred_element_type=F32) + slope_c * cpos
        m, l, p = partial(s)
        m_all = jnp.where(lane_n == j, m, m_all)
        l_all = jnp.where(lane_n == j, l, l_all)
        pv = lax.dot_general(p, pages(v_refs, blk).astype(BF16), (((1,), (1,)), ((), ())),
                             preferred_element_type=F32)
        o_s[j] = _fold_heads(pv)
        ksum = jnp.sum(kt, axis=1, keepdims=True)
        g_s[pl.ds(j, 1), :] = jnp.sum(ksum * qf_s[...], axis=0, keepdims=True) * (1.0 / MOBA_BLOCK)
    m_s[...] = m_all
    l_s[...] = l_all

    @pl.when(step == nb // bps - 1)
    def _():
        nq = kn_ref.shape[1]
        pad = jnp.zeros((LANES - nq, d), F32)
        kn = jnp.concatenate([kn_ref[0], pad], axis=0).astype(BF16)
        vn = jnp.concatenate([vn_ref[0], pad], axis=0).astype(BF16)
        kpos = lax.broadcasted_iota(jnp.int32, (1, LANES), 1).astype(F32)
        qoff_c = qoff_c_ref[...]
        s_own = lax.dot_general(qs, kn, (((1,), (1,)), ((), ())), preferred_element_type=F32) \
            - slope_c * (qoff_c - kpos)
        s_own = jnp.where(kpos <= qoff_c, s_own, NEG_INF)
        m_own, l_own, p_own = partial(s_own)
        put_column(m_s, m_own, nb)
        put_column(l_s, l_own, nb)
        o_own = _fold_heads(jnp.dot(p_own, vn, preferred_element_type=F32))
        m_r = m_s[...].T
        l_r = l_s[...].T
        gate = jnp.concatenate([g_s[...], jnp.zeros((LANES - nb, n), F32)], axis=0)
        jidx = lax.broadcasted_iota(jnp.int32, (LANES, n), 0)
        sel = jnp.logical_or(_select_topk(gate, nb), jidx == nb)
        past_len = float(nb * MOBA_BLOCK)
        bias = slope_r_ref[...] * (past_len + qoff_r_ref[...] - (jidx * MOBA_BLOCK).astype(F32))
        mj = jnp.where(sel, m_r - jnp.where(jidx == nb, 0.0, bias), -jnp.inf)
        m_tot = jnp.max(mj, axis=0, keepdims=True)
        w = jnp.where(sel, jnp.exp(mj - m_tot), 0.0)
        l_tot = jnp.sum(w * l_r, axis=0, keepdims=True)
        w_t = (w / l_tot).T

        def column(jj):
            return jnp.sum(jnp.where(lane_n == jj, w_t, 0.0), axis=1, keepdims=True)

        acc = lax.fori_loop(0, nb, lambda jj, a: a + column(jj) * o_s[jj],
                            column(nb) * o_own, unroll=SUBLANES)
        lane = lax.broadcasted_iota(jnp.int32, (nq, LANES), 1)
        tiles = []
        for t in range(N_HEADS // 2):
            ev = acc[SUBLANES * 2 * t:SUBLANES * (2 * t + 1), :]
            od = acc[SUBLANES * (2 * t + 1):SUBLANES * (2 * t + 2), :]
            tiles.append(jnp.where(lane < HEAD_DIM, ev, od))
        o_ref[0] = jnp.concatenate(tiles, axis=1)


def _sample_attn(page_table, cache_kt, cache_vt, layer, q, k_new, v_new, slopes):
    b, n_pages = page_table.shape
    ppb = MOBA_BLOCK // PAGE_SIZE
    nb = n_pages // ppb
    assert n_pages % ppb == 0 and ppb == 2 and nb < LANES and nb % SUBLANES == 0
    lq, d = k_new.shape[1], k_new.shape[2]
    n = N_HEADS * lq
    assert lq == SUBLANES and n == LANES
    slope_row = jnp.repeat(slopes, lq)[None, :]
    qoff_row = jnp.tile(jnp.arange(lq, dtype=F32), N_HEADS)[None, :]
    bps = SAMPLE_BLOCKS_PER_STEP if nb % SAMPLE_BLOCKS_PER_STEP == 0 else 1
    npg = bps * ppb
    page = lambda off: pl.BlockSpec(
        (1, 1, N_HEADS, HEAD_DIM, PAGE_SIZE),
        lambda s, j, pt: (layer, pt[s, npg * j + off], 0, 0, 0))
    per_seq = lambda r, c: pl.BlockSpec((1, r, c), lambda s, j, pt: (s, 0, 0))
    col = pl.BlockSpec((n, 1), lambda s, j, pt: (0, 0))
    row = pl.BlockSpec((1, n), lambda s, j, pt: (0, 0))
    pages = [page(off) for off in range(npg)]
    grid_spec = pltpu.PrefetchScalarGridSpec(
        num_scalar_prefetch=1,
        grid=(b, nb // bps),
        in_specs=pages + pages + [
            per_seq(lq, d), per_seq(lq, d), per_seq(lq, d), col, col, row, row],
        out_specs=per_seq(lq, d),
        scratch_shapes=[pltpu.VMEM((nb, n), F32),
                        pltpu.VMEM((n, LANES), F32),
                        pltpu.VMEM((n, LANES), F32),
                        pltpu.VMEM((nb, n, LANES), F32),
                        pltpu.VMEM((n, d), BF16),
                        pltpu.VMEM((d, n), F32)],
    )
    return pl.pallas_call(
        functools.partial(_sample_attn_kernel, bps),
        grid_spec=grid_spec,
        out_shape=jax.ShapeDtypeStruct((b, lq, d), F32),
        compiler_params=_cparams("arbitrary", "arbitrary"),
    )(page_table, *([cache_kt] * npg), *([cache_vt] * npg), q, k_new, v_new,
      slope_row.T, qoff_row.T, slope_row, qoff_row)


def _l1_out_kernel(transposed, x_ref, o_ref, sz_ref, mod_ref, g_ref, w_ref, y_ref):
    ns, tl, d = x_ref.shape
    rows = ns * tl
    if transposed:
        o = o_ref[...].reshape(d, rows).T
    else:
        o = o_ref[...].reshape(rows, d)
    a = (o * sz_ref[...].reshape(rows, d)).astype(BF16)
    out = jnp.dot(a, w_ref[...], preferred_element_type=F32).reshape(ns, tl, d)
    gate = mod_ref[...][:, :, 2 * d:3 * d]
    y_ref[...] = x_ref[...] + (1.0 + gate) * _rms(out, g_ref[...])


def _l1_out(x, o, sz, mod, g, w_out, tl, transposed):
    b, l, d = x.shape
    row = pl.BlockSpec((b, tl, d), lambda i: (0, i, 0))
    if transposed:
        o_spec = pl.BlockSpec((N_HEADS, HEAD_DIM, tl), lambda i: (0, 0, i))
    else:
        o_spec = row
    return pl.pallas_call(
        functools.partial(_l1_out_kernel, transposed),
        grid=(l // tl,),
        in_specs=[row, o_spec, row,
                  pl.BlockSpec((b, 1, 3 * d), lambda i: (0, 0, 0)),
                  pl.BlockSpec((1, d), lambda i: (0, 0)),
                  pl.BlockSpec(w_out.shape, lambda i: (0, 0))],
        out_specs=row,
        out_shape=jax.ShapeDtypeStruct((b, l, d), F32),
        compiler_params=_cparams("arbitrary"),
    )(x, o, sz, mod, g.reshape(1, d), w_out)


def _s5_weights(lbr, lbi, bbr, bbi, c_re, c_im):
    g = c_re.shape[0]
    hh, p = SSM_GROUP, SSM_STATE
    oct_n = g // 8
    lam_r = lbr.reshape(g, hh, p)[:, 0, :].reshape(oct_n, 8 * p)
    lam_i = lbi.reshape(g, hh, p)[:, 0, :].reshape(oct_n, 8 * p)
    lre = jnp.concatenate([lam_r, lam_r], axis=0)
    lim = jnp.concatenate([-lam_i, lam_i], axis=0)
    eye = jnp.eye(8, dtype=F32)

    def in_w(bb):
        bb = bb.reshape(oct_n, 8, hh, p)
        return jnp.einsum('oghp,gk->oghkp', bb, eye).reshape(oct_n, 8 * hh, 8 * p)

    def out_w(cc):
        cc = cc.reshape(oct_n, 8, hh, p)
        return jnp.einsum('oghp,gk->ogpkh', cc, eye).reshape(oct_n, 8 * p, 8 * hh)

    wb = jnp.concatenate([in_w(bbr), in_w(bbi)], axis=0).astype(BF16)
    wc = jnp.concatenate([out_w(c_re), -out_w(c_im)], axis=1).astype(BF16)
    return lre, lim, wb, wc


def _state_tiles(re, im):
    b = re.shape[0]
    return jnp.concatenate([re.reshape(b, 4, -1), im.reshape(b, 4, -1)], axis=1)


def kernel(x_prompt, x_sample, state_conv, state_ssm_re, state_ssm_im, cache_k, cache_v, page_table,
           c_prompt, c_sample, norm_pre, norm_post, ada_w, ada_b, w_in_even, conv_w, conv_b,
           ssm_lambda_re, ssm_lambda_im, ssm_log_dt, ssm_b_re, ssm_b_im, ssm_c_re, ssm_c_im,
           ssm_d, ssm_glu_w, w_out_even, w_in_odd, w_out_odd):
    bp, lp, d = x_prompt.shape
    bs, ls, _ = x_sample.shape
    g, p = ssm_lambda_re.shape[1], ssm_lambda_re.shape[2]
    n_pool = cache_k.shape[1]
    assert bp == 1 and ls == SUBLANES and d == N_HEADS * HEAD_DIM and g == 32 and p == SSM_STATE

    n_c = bp + bs
    c_all = jnp.concatenate(
        [c_prompt, c_sample, jnp.zeros((-n_c % SUBLANES, d), F32)], axis=0)
    mod = _adaln(c_all, ada_w, ada_b)
    mod_p = [mod[l, 0:bp][:, None, :] for l in range(2)]
    mod_s = [mod[l, bp:n_c][:, None, :] for l in range(2)]

    lbr, lbi, bbr, bbi = _s5_params(ssm_lambda_re[0], ssm_lambda_im[0], ssm_log_dt[0],
                                    ssm_b_re[0], ssm_b_im[0])
    lre, lim, wb, wc = _s5_weights(lbr, lbi, bbr, bbi, ssm_c_re[0], ssm_c_im[0])
    w_in0 = w_in_even[0].astype(BF16)
    glu_w = ssm_glu_w[0].astype(BF16)
    w_out0 = w_out_even[0].astype(BF16)
    wcv = conv_w.shape[2]

    def layer0(x, mods, conv0, h0_tiles, tl, tt):
        ya, u, sz, cbuf = _l0_in(x, mods, norm_pre[0], w_in0, conv_w[0], conv_b[0], conv0, tl)
        ys, h_last = _s5(u, h0_tiles, lre, lim, wb, wc, ssm_d[0], tt)
        x1 = _l0_out(x, ya, ys, sz, mods, norm_post[0], glu_w, w_out0, tl)
        b = x.shape[0]
        h_re = h_last[:, 0:4].reshape(b, g, p)
        h_im = h_last[:, 4:8].reshape(b, g, p)
        return x1, cbuf, h_re, h_im

    x1_p, conv_p, hre_p, him_p = layer0(
        x_prompt, mod_p[0], jnp.zeros((bp, 2, wcv), F32),
        jnp.zeros((bp, SUBLANES, 2 * g * p // SUBLANES), F32), 512, 256)
    x1_s, conv_s, hre_s, him_s = layer0(
        x_sample, mod_s[0], state_conv[0],
        _state_tiles(state_ssm_re[0], state_ssm_im[0]), ls, ls)

    w_in1 = w_in_odd[0].astype(BF16)
    w_out1 = w_out_odd[0].astype(BF16)
    slopes = jnp.exp2(-8.0 * jnp.arange(1, N_HEADS + 1, dtype=F32) / N_HEADS)

    k_p, v_p, sz_p, qt, kaug, vt, kbar = _l1_in(x1_p, mod_p[1], norm_pre[1], w_in1, MOBA_BLOCK, True)
    nt = lp // MOBA_BLOCK
    kbar_h = kbar.reshape(nt, N_HEADS, HEAD_DIM).transpose(1, 0, 2)
    kbar_h = jnp.pad(kbar_h, ((0, 0), (0, MAX_BLOCKS - nt), (0, 0)))
    slope_t = jnp.broadcast_to(slopes[:, None, None], (N_HEADS, SUBLANES, LANES))
    kaug_g = kaug.reshape(N_HEADS, nt // KV_GROUP, KV_GROUP * MOBA_BLOCK, kaug.shape[3])
    o_t = _attn(qt, kbar_h, slope_t, kaug_g, vt, min(Q_TILE, KV_GROUP * MOBA_BLOCK))
    y_p = _l1_out(x1_p, o_t, sz_p, mod_p[1], norm_post[1], w_out1, MOBA_BLOCK, True)

    k_s, v_s, sz_s, q_s = _l1_in(x1_s, mod_s[1], norm_pre[1], w_in1, ls, False)
    ckt = cache_k.transpose(0, 1, 3, 4, 2)
    cvt = cache_v.transpose(0, 1, 3, 4, 2)
    o_s = _sample_attn(page_table, ckt, cvt, 0, q_s, k_s, v_s, slopes)
    y_s = _l1_out(x1_s, o_s, sz_s, mod_s[1], norm_post[1], w_out1, ls, False)

    heads = lambda t: t.reshape(1, t.shape[0], t.shape[1], N_HEADS, HEAD_DIM)
    return (y_p, y_s, conv_p[None], conv_s[None],
            hre_p[None], him_p[None], hre_s[None], him_s[None],
            heads(k_p), heads(v_p), heads(k_s), heads(v_s))
```

```python
import functools

import jax
import jax.numpy as jnp
from jax import lax
from jax.experimental import pallas as pl
from jax.experimental.pallas import tpu as pltpu

F32 = jnp.float32
BF16 = jnp.bfloat16
HIGHEST = lax.Precision.HIGHEST

EPS = 1e-6
NEG_INF = -1e30
N_HEADS = 16
HEAD_DIM = 64
MOBA_BLOCK = 256
MOBA_TOPK = 3
PAGE_SIZE = 128
SSM_GROUP = 16
SSM_STATE = 64
MAX_BLOCKS = 64
KV_GROUP = 4
Q_TILE = 512
V_ROWS = 80
SAMPLE_BLOCKS_PER_STEP = 4
LOG2E = 1.4426950408889634
LANES = 128
SUBLANES = 8
VMEM_LIMIT = 48 * 1024 * 1024


def _cparams(*sem):
    return pltpu.CompilerParams(dimension_semantics=sem, vmem_limit_bytes=VMEM_LIMIT)


def _silu(x):
    return x * (1.0 / (1.0 + jnp.exp(-x)))


def _sigmoid(x):
    return 1.0 / (1.0 + jnp.exp(-x))


def _gelu_tanh(x):
    c = 0.7978845608028654
    return 0.5 * x * (1.0 + jnp.tanh(c * (x + 0.044715 * (x * x * x))))


def _rms(x, g):
    ms = jnp.mean(x * x, axis=-1, keepdims=True)
    return x * lax.rsqrt(ms + EPS) * g


def _adaln_kernel(c_ref, w_ref, b_ref, o_ref):
    c = c_ref[...]
    o_ref[0] = jnp.dot(_silu(c), w_ref[0], precision=HIGHEST,
                       preferred_element_type=F32) + b_ref[0]


def _adaln(c_all, ada_w, ada_b):
    depth, d, d3 = ada_w.shape
    r = c_all.shape[0]
    tn = 1024
    return pl.pallas_call(
        _adaln_kernel,
        grid=(depth, d3 // tn),
        in_specs=[pl.BlockSpec((r, d), lambda l, n: (0, 0)),
                  pl.BlockSpec((1, d, tn), lambda l, n: (l, 0, n)),
                  pl.BlockSpec((1, 1, tn), lambda l, n: (l, 0, n))],
        out_specs=pl.BlockSpec((1, r, tn), lambda l, n: (l, 0, n)),
        out_shape=jax.ShapeDtypeStruct((depth, r, d3), F32),
        compiler_params=_cparams("arbitrary", "arbitrary"),
    )(c_all, ada_w, ada_b.reshape(depth, 1, d3))


def _l0_in_kernel(x_ref, mod_ref, g_ref, w_ref, cw_ref, cb_ref, c0_ref,
                  ya_ref, u_ref, sz_ref, cbuf_ref, fbuf):
    ns, tl, d = x_ref.shape
    w = cw_ref.shape[1]

    @pl.when(pl.program_id(0) == 0)
    def _():
        fbuf[:, 6:8, :] = c0_ref[...]

    mod = mod_ref[...]
    shift, scale = mod[:, :, 0:d], mod[:, :, d:2 * d]
    h = _rms(x_ref[...], g_ref[...]) * (1.0 + scale) + shift
    proj = jnp.dot(h.reshape(ns * tl, d).astype(BF16), w_ref[...],
                   preferred_element_type=F32)
    xa, ba, ca = proj[:, 0:w], proj[:, w:2 * w], proj[:, 2 * w:3 * w]
    za, us, zs = proj[:, 3 * w:4 * w], proj[:, 4 * w:5 * w], proj[:, 5 * w:6 * w]
    f = (ca * xa).reshape(ns, tl, w)
    fbuf[:, 8:8 + tl, :] = f
    f1 = fbuf[:, 7:7 + tl, :]
    f2 = fbuf[:, 6:6 + tl, :]
    cw = cw_ref[...]
    conv = cb_ref[...] + cw[0:1] * f2 + cw[1:2] * f1 + cw[2:3] * f
    ya = ba.reshape(ns, tl, w) * conv * _silu(za).reshape(ns, tl, w)
    ya_ref[...] = ya
    u_ref[...] = us.reshape(ns, tl, w)
    sz_ref[...] = _silu(zs).reshape(ns, tl, w)
    tail = fbuf[:, tl + 6:tl + 8, :]
    cbuf_ref[...] = tail
    fbuf[:, 6:8, :] = tail


def _l0_in(x, mod, g, w_in, conv_w, conv_b, conv0, tl):
    b, l, d = x.shape
    w = conv_w.shape[1]
    ns = b
    return pl.pallas_call(
        _l0_in_kernel,
        grid=(l // tl,),
        in_specs=[pl.BlockSpec((ns, tl, d), lambda i: (0, i, 0)),
                  pl.BlockSpec((ns, 1, 3 * d), lambda i: (0, 0, 0)),
                  pl.BlockSpec((1, d), lambda i: (0, 0)),
                  pl.BlockSpec(w_in.shape, lambda i: (0, 0)),
                  pl.BlockSpec(conv_w.shape, lambda i: (0, 0)),
                  pl.BlockSpec((1, w), lambda i: (0, 0)),
                  pl.BlockSpec((ns, 2, w), lambda i: (0, 0, 0))],
        out_specs=[pl.BlockSpec((ns, tl, w), lambda i: (0, i, 0)),
                   pl.BlockSpec((ns, tl, w), lambda i: (0, i, 0)),
                   pl.BlockSpec((ns, tl, w), lambda i: (0, i, 0)),
                   pl.BlockSpec((ns, 2, w), lambda i: (0, 0, 0))],
        out_shape=[jax.ShapeDtypeStruct((b, l, w), F32),
                   jax.ShapeDtypeStruct((b, l, w), F32),
                   jax.ShapeDtypeStruct((b, l, w), F32),
                   jax.ShapeDtypeStruct((b, 2, w), F32)],
        scratch_shapes=[pltpu.VMEM((ns, tl + 8, w), F32)],
        compiler_params=_cparams("arbitrary"),
    )(x, mod, g.reshape(1, d), w_in, conv_w, conv_b.reshape(1, w), conv0)


def _s5_param_kernel(lr_ref, li_ref, ldt_ref, br_ref, bi_ref,
                     lbr_ref, lbi_ref, bbr_ref, bbi_ref):
    lr, li = lr_ref[...], li_ref[...]
    dt = jnp.exp(ldt_ref[...])
    mag = jnp.exp(lr * dt)
    ang = li * dt
    lbr = mag * jnp.cos(ang)
    lbi = mag * jnp.sin(ang)
    nr, ni = lbr - 1.0, lbi
    den = lr * lr + li * li
    cr = (nr * lr + ni * li) / den
    ci = (ni * lr - nr * li) / den
    br, bi = br_ref[...], bi_ref[...]
    lbr_ref[...] = lbr
    lbi_ref[...] = lbi
    bbr_ref[...] = cr * br - ci * bi
    bbi_ref[...] = cr * bi + ci * br


def _s5_params(lam_re, lam_im, log_dt, b_re, b_im):
    g, p = lam_re.shape
    hh = b_re.shape[2]
    rows = g * hh
    rep = lambda a: jnp.repeat(a, hh, axis=0)
    ldt = jnp.broadcast_to(log_dt[:, None], (g, p))
    br = b_re.transpose(0, 2, 1).reshape(rows, p)
    bi = b_im.transpose(0, 2, 1).reshape(rows, p)
    spec = pl.BlockSpec((rows, p), lambda: (0, 0))
    return pl.pallas_call(
        _s5_param_kernel,
        in_specs=[spec] * 5,
        out_specs=[spec] * 4,
        out_shape=[jax.ShapeDtypeStruct((rows, p), F32)] * 4,
    )(rep(lam_re), rep(lam_im), rep(ldt), br, bi)


def _s5_kernel(u_ref, h0_ref, lre_ref, lim_ref, wb_ref, wc_ref, d_ref,
               y_ref, hl_ref, hs, scr):
    tt = u_ref.shape[1]
    oct_w = wb_ref.shape[1]

    @pl.when(pl.program_id(1) == 0)
    def _():
        hs[...] = h0_ref[0]

    u = u_ref[0]
    ub = u.astype(BF16)
    nlt = scr.shape[0]
    for j in range(SUBLANES):
        o = j % 4
        bu = jnp.dot(ub[:, oct_w * o:oct_w * (o + 1)], wb_ref[j], preferred_element_type=F32)
        for c in range(nlt):
            scr[c, pl.ds(j, tt, stride=SUBLANES), :] = bu[:, LANES * c:LANES * (c + 1)]
    lre = [lre_ref[:, LANES * c:LANES * (c + 1)] for c in range(nlt)]
    lim = [lim_ref[:, LANES * c:LANES * (c + 1)] for c in range(nlt)]

    def body(i, hc):
        r = pl.multiple_of(i * SUBLANES, SUBLANES)
        out = []
        for c in range(nlt):
            h = lre[c] * hc[c] + lim[c] * pltpu.roll(hc[c], 4, 0) + scr[c, pl.ds(r, SUBLANES), :]
            scr[c, pl.ds(r, SUBLANES), :] = h
            out.append(h)
        return tuple(out)

    h0 = tuple(hs[:, LANES * c:LANES * (c + 1)] for c in range(nlt))
    hc = lax.fori_loop(0, tt, body, h0, unroll=8)
    h = jnp.concatenate(hc, axis=1)
    hs[...] = h
    hl_ref[0] = h
    ys = []
    for o in range(4):
        parts = [scr[c, pl.ds(part + o, tt, stride=SUBLANES), :]
                 for part in (0, 4) for c in range(nlt)]
        hcat = jnp.concatenate(parts, axis=1).astype(BF16)
        ys.append(jnp.dot(hcat, wc_ref[o], preferred_element_type=F32))
    y_ref[0] = jnp.concatenate(ys, axis=1) + d_ref[...] * u


def _s5(u, h0_tiles, lre, lim, wb, wc, d, tt):
    b, l, w = u.shape
    sw = lre.shape[1]
    return pl.pallas_call(
        _s5_kernel,
        grid=(b, l // tt),
        in_specs=[pl.BlockSpec((1, tt, w), lambda s, t: (s, t, 0)),
                  pl.BlockSpec((1, SUBLANES, sw), lambda s, t: (s, 0, 0)),
                  pl.BlockSpec(lre.shape, lambda s, t: (0, 0)),
                  pl.BlockSpec(lim.shape, lambda s, t: (0, 0)),
                  pl.BlockSpec(wb.shape, lambda s, t: (0, 0, 0)),
                  pl.BlockSpec(wc.shape, lambda s, t: (0, 0, 0)),
                  pl.BlockSpec((1, w), lambda s, t: (0, 0))],
        out_specs=[pl.BlockSpec((1, tt, w), lambda s, t: (s, t, 0)),
                   pl.BlockSpec((1, SUBLANES, sw), lambda s, t: (s, 0, 0))],
        out_shape=[jax.ShapeDtypeStruct((b, l, w), F32),
                   jax.ShapeDtypeStruct((b, SUBLANES, sw), F32)],
        scratch_shapes=[pltpu.VMEM((SUBLANES, sw), F32),
                        pltpu.VMEM((sw // LANES, tt * SUBLANES, LANES), F32)],
        compiler_params=_cparams("arbitrary", "arbitrary"),
    )(u, h0_tiles, lre, lim, wb, wc, d.reshape(1, w))


def _l0_out_kernel(x_ref, ya_ref, ys_ref, sz_ref, mod_ref, g_ref, glu_ref, w_ref, o_ref):
    ns, tl, d = x_ref.shape
    w = ya_ref.shape[2]
    rows = ns * tl
    g1 = _gelu_tanh(ys_ref[...].reshape(rows, w))
    lin = jnp.dot(g1.astype(BF16), glu_ref[...], preferred_element_type=F32)
    ys = g1 * _sigmoid(lin) * sz_ref[...].reshape(rows, w)
    cat = jnp.concatenate([ya_ref[...].reshape(rows, w).astype(BF16), ys.astype(BF16)], axis=1)
    out = jnp.dot(cat, w_ref[...], preferred_element_type=F32).reshape(ns, tl, d)
    gate = mod_ref[...][:, :, 2 * d:3 * d]
    o_ref[...] = x_ref[...] + (1.0 + gate) * _rms(out, g_ref[...])


def _l0_out(x, ya, ys, sz, mod, g, glu_w, w_out, tl):
    b, l, d = x.shape
    w = ya.shape[2]
    row = lambda width: pl.BlockSpec((b, tl, width), lambda i: (0, i, 0))
    return pl.pallas_call(
        _l0_out_kernel,
        grid=(l // tl,),
        in_specs=[row(d), row(w), row(w), row(w),
                  pl.BlockSpec((b, 1, 3 * d), lambda i: (0, 0, 0)),
                  pl.BlockSpec((1, d), lambda i: (0, 0)),
                  pl.BlockSpec(glu_w.shape, lambda i: (0, 0)),
                  pl.BlockSpec(w_out.shape, lambda i: (0, 0))],
        out_specs=row(d),
        out_shape=jax.ShapeDtypeStruct((b, l, d), F32),
        compiler_params=_cparams("arbitrary"),
    )(x, ya, ys, sz, mod, g.reshape(1, d), glu_w, w_out)


def _l1_in_kernel(prompt, x_ref, mod_ref, g_ref, w_ref, k_ref, v_ref, sz_ref, *rest):
    ns, tl, d = x_ref.shape
    rows = ns * tl
    mod = mod_ref[...]
    shift, scale = mod[:, :, 0:d], mod[:, :, d:2 * d]
    h = _rms(x_ref[...], g_ref[...]) * (1.0 + scale) + shift
    proj = jnp.dot(h.reshape(rows, d).astype(BF16), w_ref[...], preferred_element_type=F32)
    q, k, v, z = (proj[:, i * d:(i + 1) * d] for i in range(4))
    k_ref[...] = k.reshape(ns, tl, d)
    v_ref[...] = v.reshape(ns, tl, d)
    sz_ref[...] = _silu(z).reshape(ns, tl, d)
    if not prompt:
        (q_ref,) = rest
        q_ref[...] = q.reshape(ns, tl, d)
        return
    qt_ref, kaug_ref, vt_ref, kbar_ref = rest
    i = pl.program_id(0)
    qt_ref[...] = q.T.reshape(N_HEADS, HEAD_DIM, rows)
    ones_rows = (lax.broadcasted_iota(jnp.int32, (N_HEADS, V_ROWS - HEAD_DIM, rows), 1) == 0)
    vt = jnp.concatenate([v.T.reshape(N_HEADS, HEAD_DIM, rows), ones_rows.astype(F32)], axis=1)
    vt_ref[...] = vt.reshape(N_HEADS, 1, V_ROWS, rows).astype(BF16)
    kbar_ref[0] = jnp.mean(k, axis=0, keepdims=True)
    lane = lax.broadcasted_iota(jnp.int32, (rows, LANES), 1)
    rowi = lax.broadcasted_iota(jnp.int32, (rows, LANES), 0)
    onehot = (lane - HEAD_DIM == i).astype(F32)
    tile1 = jnp.where(lane < MAX_BLOCKS, (lane == i).astype(F32),
                      jnp.where(lane < MAX_BLOCKS + 2, rowi.astype(F32), 0.0)).astype(BF16)
    for hd in range(N_HEADS):
        kt = k[:, LANES * (hd // 2):LANES * (hd // 2 + 1)]
        if hd % 2:
            kt = pltpu.roll(kt, HEAD_DIM, 1)
        tile0 = jnp.where(lane < HEAD_DIM, kt, onehot).astype(BF16)
        kaug_ref[hd, 0] = jnp.concatenate([tile0, tile1], axis=1)


def _l1_in(x, mod, g, w_in, tl, prompt):
    b, l, d = x.shape
    nt = l // tl
    row = pl.BlockSpec((b, tl, d), lambda i: (0, i, 0))
    out_specs = [row, row, row]
    out_shape = [jax.ShapeDtypeStruct((b, l, d), F32)] * 3
    if prompt:
        assert b == 1 and tl == MOBA_BLOCK and nt <= MAX_BLOCKS and nt % KV_GROUP == 0
        out_specs += [pl.BlockSpec((N_HEADS, HEAD_DIM, tl), lambda i: (0, 0, i)),
                      pl.BlockSpec((N_HEADS, 1, tl, 2 * LANES), lambda i: (0, i, 0, 0)),
                      pl.BlockSpec((N_HEADS, 1, V_ROWS, tl),
                                   lambda i: (0, i // KV_GROUP, 0, i % KV_GROUP)),
                      pl.BlockSpec((1, 1, d), lambda i: (i, 0, 0))]
        out_shape += [jax.ShapeDtypeStruct((N_HEADS, HEAD_DIM, l), F32),
                      jax.ShapeDtypeStruct((N_HEADS, nt, tl, 2 * LANES), BF16),
                      jax.ShapeDtypeStruct((N_HEADS, nt // KV_GROUP, V_ROWS, KV_GROUP * tl), BF16),
                      jax.ShapeDtypeStruct((nt, 1, d), F32)]
    else:
        out_specs += [row]
        out_shape += [jax.ShapeDtypeStruct((b, l, d), F32)]
    return pl.pallas_call(
        functools.partial(_l1_in_kernel, prompt),
        grid=(nt,),
        in_specs=[row,
                  pl.BlockSpec((b, 1, 3 * d), lambda i: (0, 0, 0)),
                  pl.BlockSpec((1, d), lambda i: (0, 0)),
                  pl.BlockSpec(w_in.shape, lambda i: (0, 0))],
        out_specs=out_specs,
        out_shape=out_shape,
        compiler_params=_cparams("arbitrary"),
    )(x, mod, g.reshape(1, d), w_in)


def _select_topk(gate, n_past):
    nb = gate.shape[0]
    jidx = lax.broadcasted_iota(jnp.int32, gate.shape, 0)
    jf = jidx.astype(F32)
    past = jidx < n_past
    gm = jnp.where(past, gate, -jnp.inf)
    sel = None
    for _ in range(MOBA_TOPK):
        m = jnp.max(gm, axis=0, keepdims=True)
        idx = jnp.min(jnp.where(gm == m, jf, float(nb)), axis=0, keepdims=True)
        pick = jf == idx
        sel = pick if sel is None else jnp.logical_or(sel, pick)
        gm = jnp.where(pick, -jnp.inf, gm)
    return jnp.logical_and(sel, past)


def _split_bf16(a):
    hi = a.astype(BF16)
    lo = (a - hi.astype(F32)).astype(BF16)
    return hi, lo


def _query_features(qt, kbar, slope, tile):
    tq = qt.shape[1]
    slope = slope * LOG2E
    gate = jnp.dot(kbar, qt, precision=HIGHEST, preferred_element_type=F32)
    jidx = lax.broadcasted_iota(jnp.int32, gate.shape, 0)
    lane = lax.broadcasted_iota(jnp.int32, gate.shape, 1)
    blk = tile * (tq // MOBA_BLOCK) + lane // MOBA_BLOCK
    sel = _select_topk(gate, blk)
    bias = -(slope * float(MOBA_BLOCK)) * (blk - jidx).astype(F32)
    add = jnp.where(sel, bias, jnp.where(jidx == blk, 0.0, NEG_INF))
    a_hi, a_lo = _split_bf16(add)
    s_hi, s_lo = _split_bf16(slope)
    tail = jnp.where(jidx == 0, s_hi.astype(F32), jnp.where(jidx == 1, s_lo.astype(F32), 0.0))
    qs = qt * (HEAD_DIM ** -0.5 * LOG2E)
    qa = jnp.concatenate([qs, a_hi.astype(F32), a_lo.astype(F32), tail], axis=0)
    return qa.astype(BF16)


def _attn_body(tile, qt_ref, kbar_ref, slope_ref, kaug_ref, vt_ref, o_ref, s_a, s_b):
    tq = qt_ref.shape[2]
    gk = kaug_ref.shape[2]
    slope = slope_ref[0][0:1, 0:1]
    qa = _query_features(qt_ref[0], kbar_ref[0], slope, tile)
    last = lax.div(tile * tq, gk)

    def qk(g, dst):
        dst[...] = jnp.dot(kaug_ref[0, g], qa, preferred_element_type=F32)

    def update(src, g, carry, causal):
        m, acc = carry
        s = src[...]
        if causal:
            kpos = g * gk + lax.broadcasted_iota(jnp.int32, s.shape, 0)
            qpos = tile * tq + lax.broadcasted_iota(jnp.int32, s.shape, 1)
            s = jnp.where(kpos <= qpos, s, NEG_INF)
        m_new = jnp.maximum(m, jnp.max(s, axis=0, keepdims=True))
        p = jnp.exp2(s - m_new).astype(BF16)
        acc = jnp.exp2(m - m_new) * acc + jnp.dot(vt_ref[0, g], p, preferred_element_type=F32)
        return m_new, acc

    def pair(t, carry):
        g = 2 * t
        qk(g + 1, s_b)
        carry = update(s_a, g, carry, False)
        qk(g + 2, s_a)
        return update(s_b, g + 1, carry, False)

    def tail_even(carry):
        return update(s_a, last, carry, True)

    def tail_odd(carry):
        qk(last, s_b)
        carry = update(s_a, last - 1, carry, False)
        return update(s_b, last, carry, True)

    init = (jnp.full((1, tq), -jnp.inf, F32), jnp.zeros((vt_ref.shape[2], tq), F32))
    qk(0, s_a)
    carry = lax.fori_loop(0, last // 2, pair, init)
    _, acc = lax.cond(last % 2 == 0, tail_even, tail_odd, carry)
    o_ref[0] = acc[0:HEAD_DIM] / acc[HEAD_DIM:HEAD_DIM + 1]


def _fold_heads(pv):
    rows, d = pv.shape
    r = lax.broadcasted_iota(jnp.int32, (rows, LANES), 0) // SUBLANES
    c = lax.broadcasted_iota(jnp.int32, (rows, LANES), 1) // HEAD_DIM
    out = jnp.zeros((rows, LANES), F32)
    for t in range(d // LANES):
        out = out + jnp.where(r == 2 * t + c, pv[:, LANES * t:LANES * (t + 1)], 0.0)
    return out


def _sample_body(bps, step, k_refs, v_refs, q_ref, kn_ref, vn_ref, slope_c_ref, qoff_c_ref,
                 slope_r_ref, qoff_r_ref, o_ref, g_s, m_s, l_s, o_s, qs_s, qf_s):
    ppb = MOBA_BLOCK // PAGE_SIZE
    nb = g_s.shape[0]
    n, d = qs_s.shape
    slope_c = slope_c_ref[...]
    lane_n = lax.broadcasted_iota(jnp.int32, (n, LANES), 1)

    @pl.when(step == 0)
    def _():
        m_s[...] = jnp.zeros(m_s.shape, F32)
        l_s[...] = jnp.zeros(l_s.shape, F32)
        q_rep = jnp.concatenate([q_ref[0]] * (n // q_ref.shape[1]), axis=0)
        row_h = lax.broadcasted_iota(jnp.int32, (n, d), 0) // q_ref.shape[1]
        col_h = lax.broadcasted_iota(jnp.int32, (n, d), 1) // HEAD_DIM
        q_bd = jnp.where(row_h == col_h, q_rep, 0.0)
        qs_s[...] = (q_bd * (HEAD_DIM ** -0.5)).astype(BF16)
        qf_s[...] = q_bd.T

    qs = qs_s[...]

    def put_column(ref, col, idx):
        ref[...] = jnp.where(lane_n == idx, col, ref[...])

    def partial(s):
        m = jnp.max(s, axis=1, keepdims=True)
        p = jnp.exp(s - m)
        return m, jnp.sum(p, axis=1, keepdims=True), p.astype(BF16)

    def pages(page_refs, blk):
        return jnp.concatenate([page_refs[ppb * blk + o][0, 0].reshape(d, PAGE_SIZE)
                                for o in range(ppb)], axis=1)

    cpos = lax.broadcasted_iota(jnp.int32, (1, MOBA_BLOCK), 1).astype(F32)
    m_all, l_all = m_s[...], l_s[...]
    for blk in range(bps):
        j = step * bps + blk
        kt = pages(k_refs, blk)
        s = jnp.dot(qs, kt.astype(BF16), preferred_element_type=F32) + slope_c * cpos
        m, l, p = partial(s)
        m_all = jnp.where(lane_n == j, m, m_all)
        l_all = jnp.where(lane_n == j, l, l_all)
        pv = lax.dot_general(p, pages(v_refs, blk).astype(BF16), (((1,), (1,)), ((), ())),
                             preferred_element_type=F32)
        o_s[j] = _fold_heads(pv)
        ksum = jnp.sum(kt, axis=1, keepdims=True)
        g_s[pl.ds(j, 1), :] = jnp.sum(ksum * qf_s[...], axis=0, keepdims=True) * (1.0 / MOBA_BLOCK)
    m_s[...] = m_all
    l_s[...] = l_all

    @pl.when(step == nb // bps - 1)
    def _():
        nq = kn_ref.shape[1]
        pad = jnp.zeros((LANES - nq, d), F32)
        kn = jnp.concatenate([kn_ref[0], pad], axis=0).astype(BF16)
        vn = jnp.concatenate([vn_ref[0], pad], axis=0).astype(BF16)
        kpos = lax.broadcasted_iota(jnp.int32, (1, LANES), 1).astype(F32)
        qoff_c = qoff_c_ref[...]
        s_own = lax.dot_general(qs, kn, (((1,), (1,)), ((), ())), preferred_element_type=F32) \
            - slope_c * (qoff_c - kpos)
        s_own = jnp.where(kpos <= qoff_c, s_own, NEG_INF)
        m_own, l_own, p_own = partial(s_own)
        put_column(m_s, m_own, nb)
        put_column(l_s, l_own, nb)
        o_own = _fold_heads(jnp.dot(p_own, vn, preferred_element_type=F32))
        m_r = m_s[...].T
        l_r = l_s[...].T
        gate = jnp.concatenate([g_s[...], jnp.zeros((LANES - nb, n), F32)], axis=0)
        jidx = lax.broadcasted_iota(jnp.int32, (LANES, n), 0)
        sel = jnp.logical_or(_select_topk(gate, nb), jidx == nb)
        past_len = float(nb * MOBA_BLOCK)
        bias = slope_r_ref[...] * (past_len + qoff_r_ref[...] - (jidx * MOBA_BLOCK).astype(F32))
        mj = jnp.where(sel, m_r - jnp.where(jidx == nb, 0.0, bias), -jnp.inf)
        m_tot = jnp.max(mj, axis=0, keepdims=True)
        w = jnp.where(sel, jnp.exp(mj - m_tot), 0.0)
        l_tot = jnp.sum(w * l_r, axis=0, keepdims=True)
        w_t = (w / l_tot).T

        def column(jj):
            return jnp.sum(jnp.where(lane_n == jj, w_t, 0.0), axis=1, keepdims=True)

        acc = lax.fori_loop(0, nb, lambda jj, a: a + column(jj) * o_s[jj],
                            column(nb) * o_own, unroll=SUBLANES)
        lane = lax.broadcasted_iota(jnp.int32, (nq, LANES), 1)
        tiles = []
        for t in range(N_HEADS // 2):
            ev = acc[SUBLANES * 2 * t:SUBLANES * (2 * t + 1), :]
            od = acc[SUBLANES * (2 * t + 1):SUBLANES * (2 * t + 2), :]
            tiles.append(jnp.where(lane < HEAD_DIM, ev, od))
        o_ref[0] = jnp.concatenate(tiles, axis=1)


N_PROMPT_IN = 5


def _attention_kernel(bps, nt, ns, pt_ref, *refs):
    npg = bps * (MOBA_BLOCK // PAGE_SIZE)
    p_in = refs[0:N_PROMPT_IN]
    rest = refs[N_PROMPT_IN:]
    k_refs, v_refs = rest[0:npg], rest[npg:2 * npg]
    s_in = rest[2 * npg:2 * npg + 7]
    o_t_ref, o_s_ref, s_a, s_b = rest[2 * npg + 7:2 * npg + 11]
    s_scratch = rest[2 * npg + 11:]
    p = pl.program_id(0)
    _attn_body(lax.rem(p, nt), *p_in, o_t_ref, s_a, s_b)
    _sample_body(bps, lax.rem(p, ns), k_refs, v_refs, *s_in, o_s_ref, *s_scratch)


def _attention(qt, kbar_h, slope_t, kaug_g, vt_g, tq,
               page_table, cache_kt, cache_vt, layer, q, k_new, v_new, slopes):
    nh, hd, l = qt.shape
    ng, gk, f = kaug_g.shape[1:]
    assert gk % tq == 0 and l % tq == 0
    nt = l // tq
    b, n_pages = page_table.shape
    ppb = MOBA_BLOCK // PAGE_SIZE
    nb = n_pages // ppb
    assert n_pages % ppb == 0 and ppb == 2 and nb < LANES and nb % SUBLANES == 0
    lq, d = k_new.shape[1], k_new.shape[2]
    n = N_HEADS * lq
    assert lq == SUBLANES and n == LANES
    bps = SAMPLE_BLOCKS_PER_STEP if nb % SAMPLE_BLOCKS_PER_STEP == 0 else 1
    ns = nb // bps
    npg = bps * ppb
    assert nh * nt == b * ns, "prompt tiles and sample steps must pair up one to one"
    slope_row = jnp.repeat(slopes, lq)[None, :]
    qoff_row = jnp.tile(jnp.arange(lq, dtype=F32), N_HEADS)[None, :]

    once = pl.Buffered(1)
    prompt_specs = [
        pl.BlockSpec((1, hd, tq), lambda p, pt: (p // nt, 0, p % nt)),
        pl.BlockSpec((1, MAX_BLOCKS, hd), lambda p, pt: (p // nt, 0, 0)),
        pl.BlockSpec((1, SUBLANES, LANES), lambda p, pt: (p // nt, 0, 0)),
        pl.BlockSpec((1, ng, gk, f), lambda p, pt: (p // nt, 0, 0, 0), pipeline_mode=once),
        pl.BlockSpec((1, ng, vt_g.shape[2], gk), lambda p, pt: (p // nt, 0, 0, 0),
                     pipeline_mode=once)]
    assert len(prompt_specs) == N_PROMPT_IN
    page = lambda off: pl.BlockSpec(
        (1, 1, N_HEADS, HEAD_DIM, PAGE_SIZE),
        lambda p, pt: (layer, pt[p // ns, npg * (p % ns) + off], 0, 0, 0))
    per_seq = lambda r, c: pl.BlockSpec((1, r, c), lambda p, pt: (p // ns, 0, 0))
    col = pl.BlockSpec((n, 1), lambda p, pt: (0, 0))
    row = pl.BlockSpec((1, n), lambda p, pt: (0, 0))
    pages = [page(off) for off in range(npg)]
    grid_spec = pltpu.PrefetchScalarGridSpec(
        num_scalar_prefetch=1,
        grid=(nh * nt,),
        in_specs=prompt_specs + pages + pages + [
            per_seq(lq, d), per_seq(lq, d), per_seq(lq, d), col, col, row, row],
        out_specs=[pl.BlockSpec((1, hd, tq), lambda p, pt: (p // nt, 0, p % nt)),
                   per_seq(lq, d)],
        scratch_shapes=[pltpu.VMEM((gk, tq), F32), pltpu.VMEM((gk, tq), F32),
                        pltpu.VMEM((nb, n), F32),
                        pltpu.VMEM((n, LANES), F32),
                        pltpu.VMEM((n, LANES), F32),
                        pltpu.VMEM((nb, n, LANES), F32),
                        pltpu.VMEM((n, d), BF16),
                        pltpu.VMEM((d, n), F32)],
    )
    return pl.pallas_call(
        functools.partial(_attention_kernel, bps, nt, ns),
        grid_spec=grid_spec,
        out_shape=[jax.ShapeDtypeStruct((nh, hd, l), F32),
                   jax.ShapeDtypeStruct((b, lq, d), F32)],
        compiler_params=_cparams("arbitrary"),
    )(page_table, qt, kbar_h, slope_t, kaug_g, vt_g,
      *([cache_kt] * npg), *([cache_vt] * npg), q, k_new, v_new,
      slope_row.T, qoff_row.T, slope_row, qoff_row)


def _l1_out_kernel(transposed, x_ref, o_ref, sz_ref, mod_ref, g_ref, w_ref, y_ref):
    ns, tl, d = x_ref.shape
    rows = ns * tl
    if transposed:
        o = o_ref[...].reshape(d, rows).T
    else:
        o = o_ref[...].reshape(rows, d)
    a = (o * sz_ref[...].reshape(rows, d)).astype(BF16)
    out = jnp.dot(a, w_ref[...], preferred_element_type=F32).reshape(ns, tl, d)
    gate = mod_ref[...][:, :, 2 * d:3 * d]
    y_ref[...] = x_ref[...] + (1.0 + gate) * _rms(out, g_ref[...])


def _l1_out(x, o, sz, mod, g, w_out, tl, transposed):
    b, l, d = x.shape
    row = pl.BlockSpec((b, tl, d), lambda i: (0, i, 0))
    if transposed:
        o_spec = pl.BlockSpec((N_HEADS, HEAD_DIM, tl), lambda i: (0, 0, i))
    else:
        o_spec = row
    return pl.pallas_call(
        functools.partial(_l1_out_kernel, transposed),
        grid=(l // tl,),
        in_specs=[row, o_spec, row,
                  pl.BlockSpec((b, 1, 3 * d), lambda i: (0, 0, 0)),
                  pl.BlockSpec((1, d), lambda i: (0, 0)),
                  pl.BlockSpec(w_out.shape, lambda i: (0, 0))],
        out_specs=row,
        out_shape=jax.ShapeDtypeStruct((b, l, d), F32),
        compiler_params=_cparams("arbitrary"),
    )(x, o, sz, mod, g.reshape(1, d), w_out)


def _s5_weights(lbr, lbi, bbr, bbi, c_re, c_im):
    g = c_re.shape[0]
    hh, p = SSM_GROUP, SSM_STATE
    oct_n = g // 8
    lam_r = lbr.reshape(g, hh, p)[:, 0, :].reshape(oct_n, 8 * p)
    lam_i = lbi.reshape(g, hh, p)[:, 0, :].reshape(oct_n, 8 * p)
    lre = jnp.concatenate([lam_r, lam_r], axis=0)
    lim = jnp.concatenate([-lam_i, lam_i], axis=0)
    eye = jnp.eye(8, dtype=F32)

    def in_w(bb):
        bb = bb.reshape(oct_n, 8, hh, p)
        return jnp.einsum('oghp,gk->oghkp', bb, eye).reshape(oct_n, 8 * hh, 8 * p)

    def out_w(cc):
        cc = cc.reshape(oct_n, 8, hh, p)
        return jnp.einsum('oghp,gk->ogpkh', cc, eye).reshape(oct_n, 8 * p, 8 * hh)

    wb = jnp.concatenate([in_w(bbr), in_w(bbi)], axis=0).astype(BF16)
    wc = jnp.concatenate([out_w(c_re), -out_w(c_im)], axis=1).astype(BF16)
    return lre, lim, wb, wc


def _state_tiles(re, im):
    b = re.shape[0]
    return jnp.concatenate([re.reshape(b, 4, -1), im.reshape(b, 4, -1)], axis=1)


def kernel(x_prompt, x_sample, state_conv, state_ssm_re, state_ssm_im, cache_k, cache_v, page_table,
           c_prompt, c_sample, norm_pre, norm_post, ada_w, ada_b, w_in_even, conv_w, conv_b,
           ssm_lambda_re, ssm_lambda_im, ssm_log_dt, ssm_b_re, ssm_b_im, ssm_c_re, ssm_c_im,
           ssm_d, ssm_glu_w, w_out_even, w_in_odd, w_out_odd):
    bp, lp, d = x_prompt.shape
    bs, ls, _ = x_sample.shape
    g, p = ssm_lambda_re.shape[1], ssm_lambda_re.shape[2]
    n_pool = cache_k.shape[1]
    assert bp == 1 and ls == SUBLANES and d == N_HEADS * HEAD_DIM and g == 32 and p == SSM_STATE

    n_c = bp + bs
    c_all = jnp.concatenate(
        [c_prompt, c_sample, jnp.zeros((-n_c % SUBLANES, d), F32)], axis=0)
    mod = _adaln(c_all, ada_w, ada_b)
    mod_p = [mod[l, 0:bp][:, None, :] for l in range(2)]
    mod_s = [mod[l, bp:n_c][:, None, :] for l in range(2)]

    lbr, lbi, bbr, bbi = _s5_params(ssm_lambda_re[0], ssm_lambda_im[0], ssm_log_dt[0],
                                    ssm_b_re[0], ssm_b_im[0])
    lre, lim, wb, wc = _s5_weights(lbr, lbi, bbr, bbi, ssm_c_re[0], ssm_c_im[0])
    w_in0 = w_in_even[0].astype(BF16)
    glu_w = ssm_glu_w[0].astype(BF16)
    w_out0 = w_out_even[0].astype(BF16)
    wcv = conv_w.shape[2]

    def layer0(x, mods, conv0, h0_tiles, tl, tt):
        ya, u, sz, cbuf = _l0_in(x, mods, norm_pre[0], w_in0, conv_w[0], conv_b[0], conv0, tl)
        ys, h_last = _s5(u, h0_tiles, lre, lim, wb, wc, ssm_d[0], tt)
        x1 = _l0_out(x, ya, ys, sz, mods, norm_post[0], glu_w, w_out0, tl)
        b = x.shape[0]
        h_re = h_last[:, 0:4].reshape(b, g, p)
        h_im = h_last[:, 4:8].reshape(b, g, p)
        return x1, cbuf, h_re, h_im

    x1_p, conv_p, hre_p, him_p = layer0(
        x_prompt, mod_p[0], jnp.zeros((bp, 2, wcv), F32),
        jnp.zeros((bp, SUBLANES, 2 * g * p // SUBLANES), F32), 512, 256)
    x1_s, conv_s, hre_s, him_s = layer0(
        x_sample, mod_s[0], state_conv[0],
        _state_tiles(state_ssm_re[0], state_ssm_im[0]), ls, ls)

    w_in1 = w_in_odd[0].astype(BF16)
    w_out1 = w_out_odd[0].astype(BF16)
    slopes = jnp.exp2(-8.0 * jnp.arange(1, N_HEADS + 1, dtype=F32) / N_HEADS)

    k_p, v_p, sz_p, qt, kaug, vt, kbar = _l1_in(x1_p, mod_p[1], norm_pre[1], w_in1, MOBA_BLOCK, True)
    k_s, v_s, sz_s, q_s = _l1_in(x1_s, mod_s[1], norm_pre[1], w_in1, ls, False)
    nt = lp // MOBA_BLOCK
    kbar_h = kbar.reshape(nt, N_HEADS, HEAD_DIM).transpose(1, 0, 2)
    kbar_h = jnp.pad(kbar_h, ((0, 0), (0, MAX_BLOCKS - nt), (0, 0)))
    slope_t = jnp.broadcast_to(slopes[:, None, None], (N_HEADS, SUBLANES, LANES))
    kaug_g = kaug.reshape(N_HEADS, nt // KV_GROUP, KV_GROUP * MOBA_BLOCK, kaug.shape[3])
    ckt = cache_k.transpose(0, 1, 3, 4, 2)
    cvt = cache_v.transpose(0, 1, 3, 4, 2)
    o_t, o_s = _attention(qt, kbar_h, slope_t, kaug_g, vt, min(Q_TILE, KV_GROUP * MOBA_BLOCK),
                          page_table, ckt, cvt, 0, q_s, k_s, v_s, slopes)
    y_p = _l1_out(x1_p, o_t, sz_p, mod_p[1], norm_post[1], w_out1, MOBA_BLOCK, True)
    y_s = _l1_out(x1_s, o_s, sz_s, mod_s[1], norm_post[1], w_out1, ls, False)

    heads = lambda t: t.reshape(1, t.shape[0], t.shape[1], N_HEADS, HEAD_DIM)
    return (y_p, y_s, conv_p[None], conv_s[None],
            hre_p[None], him_p[None], hre_s[None], him_s[None],
            heads(k_p), heads(v_p), heads(k_s), heads(v_s))
```

```python
import functools

import jax
import jax.numpy as jnp
from jax import lax
from jax.experimental import pallas as pl
from jax.experimental.pallas import tpu as pltpu

F32 = jnp.float32
BF16 = jnp.bfloat16
HIGHEST = lax.Precision.HIGHEST

EPS = 1e-6
NEG_INF = -1e30
N_HEADS = 16
HEAD_DIM = 64
MOBA_BLOCK = 256
MOBA_TOPK = 3
PAGE_SIZE = 128
SSM_GROUP = 16
SSM_STATE = 64
MAX_BLOCKS = 64
KV_GROUP = 4
Q_TILE = 512
V_ROWS = 80
SAMPLE_BLOCKS_PER_STEP = 4
LOG2E = 1.4426950408889634
LANES = 128
SUBLANES = 8
VMEM_LIMIT = 48 * 1024 * 1024


def _cparams(*sem):
    return pltpu.CompilerParams(dimension_semantics=sem, vmem_limit_bytes=VMEM_LIMIT)


def _silu(x):
    return x * (1.0 / (1.0 + jnp.exp(-x)))


def _sigmoid(x):
    return 1.0 / (1.0 + jnp.exp(-x))


def _gelu_tanh(x):
    c = 0.7978845608028654
    return 0.5 * x * (1.0 + jnp.tanh(c * (x + 0.044715 * (x * x * x))))


def _rms(x, g):
    ms = jnp.mean(x * x, axis=-1, keepdims=True)
    return x * lax.rsqrt(ms + EPS) * g


def _adaln_kernel(c_ref, w_ref, b_ref, o_ref):
    c = c_ref[...]
    o_ref[0] = jnp.dot(_silu(c), w_ref[0], precision=HIGHEST,
                       preferred_element_type=F32) + b_ref[0]


def _adaln(c_all, ada_w, ada_b):
    depth, d, d3 = ada_w.shape
    r = c_all.shape[0]
    tn = 1024
    return pl.pallas_call(
        _adaln_kernel,
        grid=(depth, d3 // tn),
        in_specs=[pl.BlockSpec((r, d), lambda l, n: (0, 0)),
                  pl.BlockSpec((1, d, tn), lambda l, n: (l, 0, n)),
                  pl.BlockSpec((1, 1, tn), lambda l, n: (l, 0, n))],
        out_specs=pl.BlockSpec((1, r, tn), lambda l, n: (l, 0, n)),
        out_shape=jax.ShapeDtypeStruct((depth, r, d3), F32),
        compiler_params=_cparams("arbitrary", "arbitrary"),
    )(c_all, ada_w, ada_b.reshape(depth, 1, d3))


def _l0_in_kernel(x_ref, mod_ref, g_ref, w_ref, cw_ref, cb_ref, c0_ref,
                  ya_ref, u_ref, sz_ref, cbuf_ref, fbuf):
    ns, tl, d = x_ref.shape
    w = cw_ref.shape[1]

    @pl.when(pl.program_id(0) == 0)
    def _():
        fbuf[:, 6:8, :] = c0_ref[...]

    mod = mod_ref[...]
    shift, scale = mod[:, :, 0:d], mod[:, :, d:2 * d]
    h = _rms(x_ref[...], g_ref[...]) * (1.0 + scale) + shift
    proj = jnp.dot(h.reshape(ns * tl, d).astype(BF16), w_ref[...],
                   preferred_element_type=F32)
    xa, ba, ca = proj[:, 0:w], proj[:, w:2 * w], proj[:, 2 * w:3 * w]
    za, us, zs = proj[:, 3 * w:4 * w], proj[:, 4 * w:5 * w], proj[:, 5 * w:6 * w]
    f = (ca * xa).reshape(ns, tl, w)
    fbuf[:, 8:8 + tl, :] = f
    f1 = fbuf[:, 7:7 + tl, :]
    f2 = fbuf[:, 6:6 + tl, :]
    cw = cw_ref[...]
    conv = cb_ref[...] + cw[0:1] * f2 + cw[1:2] * f1 + cw[2:3] * f
    ya = ba.reshape(ns, tl, w) * conv * _silu(za).reshape(ns, tl, w)
    ya_ref[...] = ya
    u_ref[...] = us.reshape(ns, tl, w)
    sz_ref[...] = _silu(zs).reshape(ns, tl, w)
    tail = fbuf[:, tl + 6:tl + 8, :]
    cbuf_ref[...] = tail
    fbuf[:, 6:8, :] = tail


def _l0_in(x, mod, g, w_in, conv_w, conv_b, conv0, tl):
    b, l, d = x.shape
    w = conv_w.shape[1]
    ns = b
    return pl.pallas_call(
        _l0_in_kernel,
        grid=(l // tl,),
        in_specs=[pl.BlockSpec((ns, tl, d), lambda i: (0, i, 0)),
                  pl.BlockSpec((ns, 1, 3 * d), lambda i: (0, 0, 0)),
                  pl.BlockSpec((1, d), lambda i: (0, 0)),
                  pl.BlockSpec(w_in.shape, lambda i: (0, 0)),
                  pl.BlockSpec(conv_w.shape, lambda i: (0, 0)),
                  pl.BlockSpec((1, w), lambda i: (0, 0)),
                  pl.BlockSpec((ns, 2, w), lambda i: (0, 0, 0))],
        out_specs=[pl.BlockSpec((ns, tl, w), lambda i: (0, i, 0)),
                   pl.BlockSpec((ns, tl, w), lambda i: (0, i, 0)),
                   pl.BlockSpec((ns, tl, w), lambda i: (0, i, 0)),
                   pl.BlockSpec((ns, 2, w), lambda i: (0, 0, 0))],
        out_shape=[jax.ShapeDtypeStruct((b, l, w), F32),
                   jax.ShapeDtypeStruct((b, l, w), F32),
                   jax.ShapeDtypeStruct((b, l, w), F32),
                   jax.ShapeDtypeStruct((b, 2, w), F32)],
        scratch_shapes=[pltpu.VMEM((ns, tl + 8, w), F32)],
        compiler_params=_cparams("arbitrary"),
    )(x, mod, g.reshape(1, d), w_in, conv_w, conv_b.reshape(1, w), conv0)


def _s5_param_kernel(lr_ref, li_ref, ldt_ref, br_ref, bi_ref,
                     lbr_ref, lbi_ref, bbr_ref, bbi_ref):
    lr, li = lr_ref[...], li_ref[...]
    dt = jnp.exp(ldt_ref[...])
    mag = jnp.exp(lr * dt)
    ang = li * dt
    lbr = mag * jnp.cos(ang)
    lbi = mag * jnp.sin(ang)
    nr, ni = lbr - 1.0, lbi
    den = lr * lr + li * li
    cr = (nr * lr + ni * li) / den
    ci = (ni * lr - nr * li) / den
    br, bi = br_ref[...], bi_ref[...]
    lbr_ref[...] = lbr
    lbi_ref[...] = lbi
    bbr_ref[...] = cr * br - ci * bi
    bbi_ref[...] = cr * bi + ci * br


def _s5_params(lam_re, lam_im, log_dt, b_re, b_im):
    g, p = lam_re.shape
    hh = b_re.shape[2]
    rows = g * hh
    rep = lambda a: jnp.repeat(a, hh, axis=0)
    ldt = jnp.broadcast_to(log_dt[:, None], (g, p))
    br = b_re.transpose(0, 2, 1).reshape(rows, p)
    bi = b_im.transpose(0, 2, 1).reshape(rows, p)
    spec = pl.BlockSpec((rows, p), lambda: (0, 0))
    return pl.pallas_call(
        _s5_param_kernel,
        in_specs=[spec] * 5,
        out_specs=[spec] * 4,
        out_shape=[jax.ShapeDtypeStruct((rows, p), F32)] * 4,
    )(rep(lam_re), rep(lam_im), rep(ldt), br, bi)


def _s5_kernel(u_ref, h0_ref, lre_ref, lim_ref, wb_ref, wc_ref, d_ref,
               y_ref, hl_ref, hs, scr):
    tt = u_ref.shape[1]
    oct_w = wb_ref.shape[1]

    @pl.when(pl.program_id(1) == 0)
    def _():
        hs[...] = h0_ref[0]

    u = u_ref[0]
    ub = u.astype(BF16)
    nlt = scr.shape[0]
    for j in range(SUBLANES):
        o = j % 4
        bu = jnp.dot(ub[:, oct_w * o:oct_w * (o + 1)], wb_ref[j], preferred_element_type=F32)
        for c in range(nlt):
            scr[c, pl.ds(j, tt, stride=SUBLANES), :] = bu[:, LANES * c:LANES * (c + 1)]
    lre = [lre_ref[:, LANES * c:LANES * (c + 1)] for c in range(nlt)]
    lim = [lim_ref[:, LANES * c:LANES * (c + 1)] for c in range(nlt)]

    def body(i, hc):
        r = pl.multiple_of(i * SUBLANES, SUBLANES)
        out = []
        for c in range(nlt):
            h = lre[c] * hc[c] + lim[c] * pltpu.roll(hc[c], 4, 0) + scr[c, pl.ds(r, SUBLANES), :]
            scr[c, pl.ds(r, SUBLANES), :] = h
            out.append(h)
        return tuple(out)

    h0 = tuple(hs[:, LANES * c:LANES * (c + 1)] for c in range(nlt))
    hc = lax.fori_loop(0, tt, body, h0, unroll=8)
    h = jnp.concatenate(hc, axis=1)
    hs[...] = h
    hl_ref[0] = h
    ys = []
    for o in range(4):
        parts = [scr[c, pl.ds(part + o, tt, stride=SUBLANES), :]
                 for part in (0, 4) for c in range(nlt)]
        hcat = jnp.concatenate(parts, axis=1).astype(BF16)
        ys.append(jnp.dot(hcat, wc_ref[o], preferred_element_type=F32))
    y_ref[0] = jnp.concatenate(ys, axis=1) + d_ref[...] * u


def _s5(u, h0_tiles, lre, lim, wb, wc, d, tt):
    b, l, w = u.shape
    sw = lre.shape[1]
    return pl.pallas_call(
        _s5_kernel,
        grid=(b, l // tt),
        in_specs=[pl.BlockSpec((1, tt, w), lambda s, t: (s, t, 0)),
                  pl.BlockSpec((1, SUBLANES, sw), lambda s, t: (s, 0, 0)),
                  pl.BlockSpec(lre.shape, lambda s, t: (0, 0)),
                  pl.BlockSpec(lim.shape, lambda s, t: (0, 0)),
                  pl.BlockSpec(wb.shape, lambda s, t: (0, 0, 0)),
                  pl.BlockSpec(wc.shape, lambda s, t: (0, 0, 0)),
                  pl.BlockSpec((1, w), lambda s, t: (0, 0))],
        out_specs=[pl.BlockSpec((1, tt, w), lambda s, t: (s, t, 0)),
                   pl.BlockSpec((1, SUBLANES, sw), lambda s, t: (s, 0, 0))],
        out_shape=[jax.ShapeDtypeStruct((b, l, w), F32),
                   jax.ShapeDtypeStruct((b, SUBLANES, sw), F32)],
        scratch_shapes=[pltpu.VMEM((SUBLANES, sw), F32),
                        pltpu.VMEM((sw // LANES, tt * SUBLANES, LANES), F32)],
        compiler_params=_cparams("arbitrary", "arbitrary"),
    )(u, h0_tiles, lre, lim, wb, wc, d.reshape(1, w))


def _l0_out_kernel(x_ref, ya_ref, ys_ref, sz_ref, mod_ref, g_ref, glu_ref, w_ref, o_ref):
    ns, tl, d = x_ref.shape
    w = ya_ref.shape[2]
    rows = ns * tl
    g1 = _gelu_tanh(ys_ref[...].reshape(rows, w))
    lin = jnp.dot(g1.astype(BF16), glu_ref[...], preferred_element_type=F32)
    ys = g1 * _sigmoid(lin) * sz_ref[...].reshape(rows, w)
    cat = jnp.concatenate([ya_ref[...].reshape(rows, w).astype(BF16), ys.astype(BF16)], axis=1)
    out = jnp.dot(cat, w_ref[...], preferred_element_type=F32).reshape(ns, tl, d)
    gate = mod_ref[...][:, :, 2 * d:3 * d]
    o_ref[...] = x_ref[...] + (1.0 + gate) * _rms(out, g_ref[...])


def _l0_out(x, ya, ys, sz, mod, g, glu_w, w_out, tl):
    b, l, d = x.shape
    w = ya.shape[2]
    row = lambda width: pl.BlockSpec((b, tl, width), lambda i: (0, i, 0))
    return pl.pallas_call(
        _l0_out_kernel,
        grid=(l // tl,),
        in_specs=[row(d), row(w), row(w), row(w),
                  pl.BlockSpec((b, 1, 3 * d), lambda i: (0, 0, 0)),
                  pl.BlockSpec((1, d), lambda i: (0, 0)),
                  pl.BlockSpec(glu_w.shape, lambda i: (0, 0)),
                  pl.BlockSpec(w_out.shape, lambda i: (0, 0))],
        out_specs=row(d),
        out_shape=jax.ShapeDtypeStruct((b, l, d), F32),
        compiler_params=_cparams("arbitrary"),
    )(x, ya, ys, sz, mod, g.reshape(1, d), glu_w, w_out)


def _l1_in_kernel(prompt, x_ref, mod_ref, g_ref, w_ref, k_ref, v_ref, sz_ref, *rest):
    ns, tl, d = x_ref.shape
    rows = ns * tl
    mod = mod_ref[...]
    shift, scale = mod[:, :, 0:d], mod[:, :, d:2 * d]
    h = _rms(x_ref[...], g_ref[...]) * (1.0 + scale) + shift
    proj = jnp.dot(h.reshape(rows, d).astype(BF16), w_ref[...], preferred_element_type=F32)
    q, k, v, z = (proj[:, i * d:(i + 1) * d] for i in range(4))
    k_ref[...] = k.reshape(ns, tl, d)
    v_ref[...] = v.reshape(ns, tl, d)
    sz_ref[...] = _silu(z).reshape(ns, tl, d)
    if not prompt:
        (q_ref,) = rest
        q_ref[...] = q.reshape(ns, tl, d)
        return
    qt_ref, kaug_ref, vt_ref, kbar_ref = rest
    i = pl.program_id(0)
    qt_ref[...] = q.T.reshape(N_HEADS, HEAD_DIM, rows)
    ones_rows = (lax.broadcasted_iota(jnp.int32, (N_HEADS, V_ROWS - HEAD_DIM, rows), 1) == 0)
    vt = jnp.concatenate([v.T.reshape(N_HEADS, HEAD_DIM, rows), ones_rows.astype(F32)], axis=1)
    vt_ref[...] = vt.reshape(N_HEADS, 1, V_ROWS, rows).astype(BF16)
    kbar_ref[0] = jnp.mean(k, axis=0, keepdims=True)
    lane = lax.broadcasted_iota(jnp.int32, (rows, LANES), 1)
    rowi = lax.broadcasted_iota(jnp.int32, (rows, LANES), 0)
    onehot = (lane - HEAD_DIM == i).astype(F32)
    tile1 = jnp.where(lane < MAX_BLOCKS, (lane == i).astype(F32),
                      jnp.where(lane < MAX_BLOCKS + 2, rowi.astype(F32), 0.0)).astype(BF16)
    for hd in range(N_HEADS):
        kt = k[:, LANES * (hd // 2):LANES * (hd // 2 + 1)]
        if hd % 2:
            kt = pltpu.roll(kt, HEAD_DIM, 1)
        tile0 = jnp.where(lane < HEAD_DIM, kt, onehot).astype(BF16)
        kaug_ref[hd, 0] = jnp.concatenate([tile0, tile1], axis=1)


def _l1_in(x, mod, g, w_in, tl, prompt):
    b, l, d = x.shape
    nt = l // tl
    row = pl.BlockSpec((b, tl, d), lambda i: (0, i, 0))
    out_specs = [row, row, row]
    out_shape = [jax.ShapeDtypeStruct((b, l, d), F32)] * 3
    if prompt:
        assert b == 1 and tl == MOBA_BLOCK and nt <= MAX_BLOCKS and nt % KV_GROUP == 0
        out_specs += [pl.BlockSpec((N_HEADS, HEAD_DIM, tl), lambda i: (0, 0, i)),
                      pl.BlockSpec((N_HEADS, 1, tl, 2 * LANES), lambda i: (0, i, 0, 0)),
                      pl.BlockSpec((N_HEADS, 1, V_ROWS, tl),
                                   lambda i: (0, i // KV_GROUP, 0, i % KV_GROUP)),
                      pl.BlockSpec((1, 1, d), lambda i: (i, 0, 0))]
        out_shape += [jax.ShapeDtypeStruct((N_HEADS, HEAD_DIM, l), F32),
                      jax.ShapeDtypeStruct((N_HEADS, nt, tl, 2 * LANES), BF16),
                      jax.ShapeDtypeStruct((N_HEADS, nt // KV_GROUP, V_ROWS, KV_GROUP * tl), BF16),
                      jax.ShapeDtypeStruct((nt, 1, d), F32)]
    else:
        out_specs += [row]
        out_shape += [jax.ShapeDtypeStruct((b, l, d), F32)]
    return pl.pallas_call(
        functools.partial(_l1_in_kernel, prompt),
        grid=(nt,),
        in_specs=[row,
                  pl.BlockSpec((b, 1, 3 * d), lambda i: (0, 0, 0)),
                  pl.BlockSpec((1, d), lambda i: (0, 0)),
                  pl.BlockSpec(w_in.shape, lambda i: (0, 0))],
        out_specs=out_specs,
        out_shape=out_shape,
        compiler_params=_cparams("arbitrary"),
    )(x, mod, g.reshape(1, d), w_in)


def _select_topk(gate, n_past):
    nb = gate.shape[0]
    jidx = lax.broadcasted_iota(jnp.int32, gate.shape, 0)
    jf = jidx.astype(F32)
    past = jidx < n_past
    gm = jnp.where(past, gate, -jnp.inf)
    sel = None
    for _ in range(MOBA_TOPK):
        m = jnp.max(gm, axis=0, keepdims=True)
        idx = jnp.min(jnp.where(gm == m, jf, float(nb)), axis=0, keepdims=True)
        pick = jf == idx
        sel = pick if sel is None else jnp.logical_or(sel, pick)
        gm = jnp.where(pick, -jnp.inf, gm)
    return jnp.logical_and(sel, past)


def _split_bf16(a):
    hi = a.astype(BF16)
    lo = (a - hi.astype(F32)).astype(BF16)
    return hi, lo


def _query_features(qt, kbar, slope, tile):
    tq = qt.shape[1]
    slope = slope * LOG2E
    gate = jnp.dot(kbar, qt, precision=HIGHEST, preferred_element_type=F32)
    jidx = lax.broadcasted_iota(jnp.int32, gate.shape, 0)
    lane = lax.broadcasted_iota(jnp.int32, gate.shape, 1)
    blk = tile * (tq // MOBA_BLOCK) + lane // MOBA_BLOCK
    sel = _select_topk(gate, blk)
    bias = -(slope * float(MOBA_BLOCK)) * (blk - jidx).astype(F32)
    add = jnp.where(sel, bias, jnp.where(jidx == blk, 0.0, NEG_INF))
    a_hi, a_lo = _split_bf16(add)
    s_hi, s_lo = _split_bf16(slope)
    tail = jnp.where(jidx == 0, s_hi.astype(F32), jnp.where(jidx == 1, s_lo.astype(F32), 0.0))
    qs = qt * (HEAD_DIM ** -0.5 * LOG2E)
    qa = jnp.concatenate([qs, a_hi.astype(F32), a_lo.astype(F32), tail], axis=0)
    return qa.astype(BF16)


def _attn_body(tile, qt_ref, kbar_ref, slope_ref, kaug_ref, vt_ref, o_ref, s_a, s_b):
    tq = qt_ref.shape[2]
    gk = kaug_ref.shape[2]
    slope = slope_ref[0][0:1, 0:1]
    qa = _query_features(qt_ref[0], kbar_ref[0], slope, tile)
    last = lax.div(tile * tq, gk)

    def qk(g, dst):
        dst[...] = jnp.dot(kaug_ref[0, g], qa, preferred_element_type=F32)

    def update(src, g, carry, causal):
        m, acc = carry
        s = src[...]
        if causal:
            kpos = g * gk + lax.broadcasted_iota(jnp.int32, s.shape, 0)
            qpos = tile * tq + lax.broadcasted_iota(jnp.int32, s.shape, 1)
            s = jnp.where(kpos <= qpos, s, NEG_INF)
        m_new = jnp.maximum(m, jnp.max(s, axis=0, keepdims=True))
        p = jnp.exp2(s - m_new).astype(BF16)
        acc = jnp.exp2(m - m_new) * acc + jnp.dot(vt_ref[0, g], p, preferred_element_type=F32)
        return m_new, acc

    def pair(t, carry):
        g = 2 * t
        qk(g + 1, s_b)
        carry = update(s_a, g, carry, False)
        qk(g + 2, s_a)
        return update(s_b, g + 1, carry, False)

    def tail_even(carry):
        return update(s_a, last, carry, True)

    def tail_odd(carry):
        qk(last, s_b)
        carry = update(s_a, last - 1, carry, False)
        return update(s_b, last, carry, True)

    init = (jnp.full((1, tq), -jnp.inf, F32), jnp.zeros((vt_ref.shape[2], tq), F32))
    qk(0, s_a)
    carry = lax.fori_loop(0, last // 2, pair, init)
    _, acc = lax.cond(last % 2 == 0, tail_even, tail_odd, carry)
    o_ref[0] = acc[0:HEAD_DIM] / acc[HEAD_DIM:HEAD_DIM + 1]


def _fold_heads(pv):
    rows, d = pv.shape
    r = lax.broadcasted_iota(jnp.int32, (rows, LANES), 0) // SUBLANES
    c = lax.broadcasted_iota(jnp.int32, (rows, LANES), 1) // HEAD_DIM
    out = jnp.zeros((rows, LANES), F32)
    for t in range(d // LANES):
        out = out + jnp.where(r == 2 * t + c, pv[:, LANES * t:LANES * (t + 1)], 0.0)
    return out


def _sample_body(bps, step, k_refs, v_refs, q_ref, kn_ref, vn_ref, slope_c_ref, qoff_c_ref,
                 slope_r_ref, qoff_r_ref, o_ref, g_s, m_s, l_s, o_s, qs_s, qf_s):
    ppb = MOBA_BLOCK // PAGE_SIZE
    nb = g_s.shape[0]
    n, d = qs_s.shape
    slope_c = slope_c_ref[...]
    lane_n = lax.broadcasted_iota(jnp.int32, (n, LANES), 1)

    @pl.when(step == 0)
    def _():
        m_s[...] = jnp.zeros(m_s.shape, F32)
        l_s[...] = jnp.zeros(l_s.shape, F32)
        q_rep = jnp.concatenate([q_ref[0]] * (n // q_ref.shape[1]), axis=0)
        row_h = lax.broadcasted_iota(jnp.int32, (n, d), 0) // q_ref.shape[1]
        col_h = lax.broadcasted_iota(jnp.int32, (n, d), 1) // HEAD_DIM
        q_bd = jnp.where(row_h == col_h, q_rep, 0.0)
        qs_s[...] = (q_bd * (HEAD_DIM ** -0.5)).astype(BF16)
        qf_s[...] = q_bd.T

    qs = qs_s[...]

    def put_column(ref, col, idx):
        ref[...] = jnp.where(lane_n == idx, col, ref[...])

    def partial(s):
        m = jnp.max(s, axis=1, keepdims=True)
        p = jnp.exp(s - m)
        return m, jnp.sum(p, axis=1, keepdims=True), p.astype(BF16)

    def pages(page_refs, blk):
        return jnp.concatenate([page_refs[ppb * blk + o][0, 0].reshape(d, PAGE_SIZE)
                                for o in range(ppb)], axis=1)

    cpos = lax.broadcasted_iota(jnp.int32, (1, MOBA_BLOCK), 1).astype(F32)
    m_all, l_all = m_s[...], l_s[...]
    kts = [pages(k_refs, blk) for blk in range(bps)]
    ss = [jnp.dot(qs, kt.astype(BF16), preferred_element_type=F32) + slope_c * cpos for kt in kts]
    for blk in range(bps):
        j = step * bps + blk
        ksum = jnp.sum(kts[blk], axis=1, keepdims=True)
        g_s[pl.ds(j, 1), :] = jnp.sum(ksum * qf_s[...], axis=0, keepdims=True) * (1.0 / MOBA_BLOCK)
    ps = []
    for blk in range(bps):
        j = step * bps + blk
        m, l, p = partial(ss[blk])
        m_all = jnp.where(lane_n == j, m, m_all)
        l_all = jnp.where(lane_n == j, l, l_all)
        ps.append(p)
    for blk in range(bps):
        j = step * bps + blk
        pv = lax.dot_general(ps[blk], pages(v_refs, blk).astype(BF16), (((1,), (1,)), ((), ())),
                             preferred_element_type=F32)
        o_s[j] = _fold_heads(pv)
    m_s[...] = m_all
    l_s[...] = l_all

    @pl.when(step == nb // bps - 1)
    def _():
        nq = kn_ref.shape[1]
        pad = jnp.zeros((LANES - nq, d), F32)
        kn = jnp.concatenate([kn_ref[0], pad], axis=0).astype(BF16)
        vn = jnp.concatenate([vn_ref[0], pad], axis=0).astype(BF16)
        kpos = lax.broadcasted_iota(jnp.int32, (1, LANES), 1).astype(F32)
        qoff_c = qoff_c_ref[...]
        s_own = lax.dot_general(qs, kn, (((1,), (1,)), ((), ())), preferred_element_type=F32) \
            - slope_c * (qoff_c - kpos)
        s_own = jnp.where(kpos <= qoff_c, s_own, NEG_INF)
        m_own, l_own, p_own = partial(s_own)
        put_column(m_s, m_own, nb)
        put_column(l_s, l_own, nb)
        o_own = _fold_heads(jnp.dot(p_own, vn, preferred_element_type=F32))
        m_r = m_s[...].T
        l_r = l_s[...].T
        gate = jnp.concatenate([g_s[...], jnp.zeros((LANES - nb, n), F32)], axis=0)
        jidx = lax.broadcasted_iota(jnp.int32, (LANES, n), 0)
        sel = jnp.logical_or(_select_topk(gate, nb), jidx == nb)
        past_len = float(nb * MOBA_BLOCK)
        bias = slope_r_ref[...] * (past_len + qoff_r_ref[...] - (jidx * MOBA_BLOCK).astype(F32))
        mj = jnp.where(sel, m_r - jnp.where(jidx == nb, 0.0, bias), -jnp.inf)
        m_tot = jnp.max(mj, axis=0, keepdims=True)
        w = jnp.where(sel, jnp.exp(mj - m_tot), 0.0)
        l_tot = jnp.sum(w * l_r, axis=0, keepdims=True)
        w_t = (w / l_tot).T

        def column(jj):
            return jnp.sum(jnp.where(lane_n == jj, w_t, 0.0), axis=1, keepdims=True)

        acc = lax.fori_loop(0, nb, lambda jj, a: a + column(jj) * o_s[jj],
                            column(nb) * o_own, unroll=SUBLANES)
        lane = lax.broadcasted_iota(jnp.int32, (nq, LANES), 1)
        tiles = []
        for t in range(N_HEADS // 2):
            ev = acc[SUBLANES * 2 * t:SUBLANES * (2 * t + 1), :]
            od = acc[SUBLANES * (2 * t + 1):SUBLANES * (2 * t + 2), :]
            tiles.append(jnp.where(lane < HEAD_DIM, ev, od))
        o_ref[0] = jnp.concatenate(tiles, axis=1)


N_PROMPT_IN = 5


def _attention_kernel(bps, nt, ns, pt_ref, *refs):
    npg = bps * (MOBA_BLOCK // PAGE_SIZE)
    p_in = refs[0:N_PROMPT_IN]
    rest = refs[N_PROMPT_IN:]
    k_refs, v_refs = rest[0:npg], rest[npg:2 * npg]
    s_in = rest[2 * npg:2 * npg + 7]
    o_t_ref, o_s_ref, s_a, s_b = rest[2 * npg + 7:2 * npg + 11]
    s_scratch = rest[2 * npg + 11:]
    p = pl.program_id(0)
    _attn_body(lax.rem(p, nt), *p_in, o_t_ref, s_a, s_b)
    _sample_body(bps, lax.rem(p, ns), k_refs, v_refs, *s_in, o_s_ref, *s_scratch)


def _attention(qt, kbar_h, slope_t, kaug_g, vt_g, tq,
               page_table, cache_kt, cache_vt, layer, q, k_new, v_new, slopes):
    nh, hd, l = qt.shape
    ng, gk, f = kaug_g.shape[1:]
    assert gk % tq == 0 and l % tq == 0
    nt = l // tq
    b, n_pages = page_table.shape
    ppb = MOBA_BLOCK // PAGE_SIZE
    nb = n_pages // ppb
    assert n_pages % ppb == 0 and ppb == 2 and nb < LANES and nb % SUBLANES == 0
    lq, d = k_new.shape[1], k_new.shape[2]
    n = N_HEADS * lq
    assert lq == SUBLANES and n == LANES
    bps = SAMPLE_BLOCKS_PER_STEP if nb % SAMPLE_BLOCKS_PER_STEP == 0 else 1
    ns = nb // bps
    npg = bps * ppb
    assert nh * nt == b * ns, "prompt tiles and sample steps must pair up one to one"
    slope_row = jnp.repeat(slopes, lq)[None, :]
    qoff_row = jnp.tile(jnp.arange(lq, dtype=F32), N_HEADS)[None, :]

    once = pl.Buffered(1)
    prompt_specs = [
        pl.BlockSpec((1, hd, tq), lambda p, pt: (p // nt, 0, p % nt)),
        pl.BlockSpec((1, MAX_BLOCKS, hd), lambda p, pt: (p // nt, 0, 0)),
        pl.BlockSpec((1, SUBLANES, LANES), lambda p, pt: (p // nt, 0, 0)),
        pl.BlockSpec((1, ng, gk, f), lambda p, pt: (p // nt, 0, 0, 0), pipeline_mode=once),
        pl.BlockSpec((1, ng, vt_g.shape[2], gk), lambda p, pt: (p // nt, 0, 0, 0),
                     pipeline_mode=once)]
    assert len(prompt_specs) == N_PROMPT_IN
    page = lambda off: pl.BlockSpec(
        (1, 1, N_HEADS, HEAD_DIM, PAGE_SIZE),
        lambda p, pt: (layer, pt[p // ns, npg * (p % ns) + off], 0, 0, 0))
    per_seq = lambda r, c: pl.BlockSpec((1, r, c), lambda p, pt: (p // ns, 0, 0))
    col = pl.BlockSpec((n, 1), lambda p, pt: (0, 0))
    row = pl.BlockSpec((1, n), lambda p, pt: (0, 0))
    pages = [page(off) for off in range(npg)]
    grid_spec = pltpu.PrefetchScalarGridSpec(
        num_scalar_prefetch=1,
        grid=(nh * nt,),
        in_specs=prompt_specs + pages + pages + [
            per_seq(lq, d), per_seq(lq, d), per_seq(lq, d), col, col, row, row],
        out_specs=[pl.BlockSpec((1, hd, tq), lambda p, pt: (p // nt, 0, p % nt)),
                   per_seq(lq, d)],
        scratch_shapes=[pltpu.VMEM((gk, tq), F32), pltpu.VMEM((gk, tq), F32),
                        pltpu.VMEM((nb, n), F32),
                        pltpu.VMEM((n, LANES), F32),
                        pltpu.VMEM((n, LANES), F32),
                        pltpu.VMEM((nb, n, LANES), F32),
                        pltpu.VMEM((n, d), BF16),
                        pltpu.VMEM((d, n), F32)],
    )
    return pl.pallas_call(
        functools.partial(_attention_kernel, bps, nt, ns),
        grid_spec=grid_spec,
        out_shape=[jax.ShapeDtypeStruct((nh, hd, l), F32),
                   jax.ShapeDtypeStruct((b, lq, d), F32)],
        compiler_params=_cparams("arbitrary"),
    )(page_table, qt, kbar_h, slope_t, kaug_g, vt_g,
      *([cache_kt] * npg), *([cache_vt] * npg), q, k_new, v_new,
      slope_row.T, qoff_row.T, slope_row, qoff_row)


def _l1_out_kernel(transposed, x_ref, o_ref, sz_ref, mod_ref, g_ref, w_ref, y_ref):
    ns, tl, d = x_ref.shape
    rows = ns * tl
    if transposed:
        o = o_ref[...].reshape(d, rows).T
    else:
        o = o_ref[...].reshape(rows, d)
    a = (o * sz_ref[...].reshape(rows, d)).astype(BF16)
    out = jnp.dot(a, w_ref[...], preferred_element_type=F32).reshape(ns, tl, d)
    gate = mod_ref[...][:, :, 2 * d:3 * d]
    y_ref[...] = x_ref[...] + (1.0 + gate) * _rms(out, g_ref[...])


def _l1_out(x, o, sz, mod, g, w_out, tl, transposed):
    b, l, d = x.shape
    row = pl.BlockSpec((b, tl, d), lambda i: (0, i, 0))
    if transposed:
        o_spec = pl.BlockSpec((N_HEADS, HEAD_DIM, tl), lambda i: (0, 0, i))
    else:
        o_spec = row
    return pl.pallas_call(
        functools.partial(_l1_out_kernel, transposed),
        grid=(l // tl,),
        in_specs=[row, o_spec, row,
                  pl.BlockSpec((b, 1, 3 * d), lambda i: (0, 0, 0)),
                  pl.BlockSpec((1, d), lambda i: (0, 0)),
                  pl.BlockSpec(w_out.shape, lambda i: (0, 0))],
        out_specs=row,
        out_shape=jax.ShapeDtypeStruct((b, l, d), F32),
        compiler_params=_cparams("arbitrary"),
    )(x, o, sz, mod, g.reshape(1, d), w_out)


def _s5_weights(lbr, lbi, bbr, bbi, c_re, c_im):
    g = c_re.shape[0]
    hh, p = SSM_GROUP, SSM_STATE
    oct_n = g // 8
    lam_r = lbr.reshape(g, hh, p)[:, 0, :].reshape(oct_n, 8 * p)
    lam_i = lbi.reshape(g, hh, p)[:, 0, :].reshape(oct_n, 8 * p)
    lre = jnp.concatenate([lam_r, lam_r], axis=0)
    lim = jnp.concatenate([-lam_i, lam_i], axis=0)
    eye = jnp.eye(8, dtype=F32)

    def in_w(bb):
        bb = bb.reshape(oct_n, 8, hh, p)
        return jnp.einsum('oghp,gk->oghkp', bb, eye).reshape(oct_n, 8 * hh, 8 * p)

    def out_w(cc):
        cc = cc.reshape(oct_n, 8, hh, p)
        return jnp.einsum('oghp,gk->ogpkh', cc, eye).reshape(oct_n, 8 * p, 8 * hh)

    wb = jnp.concatenate([in_w(bbr), in_w(bbi)], axis=0).astype(BF16)
    wc = jnp.concatenate([out_w(c_re), -out_w(c_im)], axis=1).astype(BF16)
    return lre, lim, wb, wc


def _state_tiles(re, im):
    b = re.shape[0]
    return jnp.concatenate([re.reshape(b, 4, -1), im.reshape(b, 4, -1)], axis=1)


def kernel(x_prompt, x_sample, state_conv, state_ssm_re, state_ssm_im, cache_k, cache_v, page_table,
           c_prompt, c_sample, norm_pre, norm_post, ada_w, ada_b, w_in_even, conv_w, conv_b,
           ssm_lambda_re, ssm_lambda_im, ssm_log_dt, ssm_b_re, ssm_b_im, ssm_c_re, ssm_c_im,
           ssm_d, ssm_glu_w, w_out_even, w_in_odd, w_out_odd):
    bp, lp, d = x_prompt.shape
    bs, ls, _ = x_sample.shape
    g, p = ssm_lambda_re.shape[1], ssm_lambda_re.shape[2]
    n_pool = cache_k.shape[1]
    assert bp == 1 and ls == SUBLANES and d == N_HEADS * HEAD_DIM and g == 32 and p == SSM_STATE

    n_c = bp + bs
    c_all = jnp.concatenate(
        [c_prompt, c_sample, jnp.zeros((-n_c % SUBLANES, d), F32)], axis=0)
    mod = _adaln(c_all, ada_w, ada_b)
    mod_p = [mod[l, 0:bp][:, None, :] for l in range(2)]
    mod_s = [mod[l, bp:n_c][:, None, :] for l in range(2)]

    lbr, lbi, bbr, bbi = _s5_params(ssm_lambda_re[0], ssm_lambda_im[0], ssm_log_dt[0],
                                    ssm_b_re[0], ssm_b_im[0])
    lre, lim, wb, wc = _s5_weights(lbr, lbi, bbr, bbi, ssm_c_re[0], ssm_c_im[0])
    w_in0 = w_in_even[0].astype(BF16)
    glu_w = ssm_glu_w[0].astype(BF16)
    w_out0 = w_out_even[0].astype(BF16)
    wcv = conv_w.shape[2]

    def layer0(x, mods, conv0, h0_tiles, tl, tt):
        ya, u, sz, cbuf = _l0_in(x, mods, norm_pre[0], w_in0, conv_w[0], conv_b[0], conv0, tl)
        ys, h_last = _s5(u, h0_tiles, lre, lim, wb, wc, ssm_d[0], tt)
        x1 = _l0_out(x, ya, ys, sz, mods, norm_post[0], glu_w, w_out0, tl)
        b = x.shape[0]
        h_re = h_last[:, 0:4].reshape(b, g, p)
        h_im = h_last[:, 4:8].reshape(b, g, p)
        return x1, cbuf, h_re, h_im

    x1_p, conv_p, hre_p, him_p = layer0(
        x_prompt, mod_p[0], jnp.zeros((bp, 2, wcv), F32),
        jnp.zeros((bp, SUBLANES, 2 * g * p // SUBLANES), F32), 512, 256)
    x1_s, conv_s, hre_s, him_s = layer0(
        x_sample, mod_s[0], state_conv[0],
        _state_tiles(state_ssm_re[0], state_ssm_im[0]), ls, ls)

    w_in1 = w_in_odd[0].astype(BF16)
    w_out1 = w_out_odd[0].astype(BF16)
    slopes = jnp.exp2(-8.0 * jnp.arange(1, N_HEADS + 1, dtype=F32) / N_HEADS)

    k_p, v_p, sz_p, qt, kaug, vt, kbar = _l1_in(x1_p, mod_p[1], norm_pre[1], w_in1, MOBA_BLOCK, True)
    k_s, v_s, sz_s, q_s = _l1_in(x1_s, mod_s[1], norm_pre[1], w_in1, ls, False)
    nt = lp // MOBA_BLOCK
    kbar_h = kbar.reshape(nt, N_HEADS, HEAD_DIM).transpose(1, 0, 2)
    kbar_h = jnp.pad(kbar_h, ((0, 0), (0, MAX_BLOCKS - nt), (0, 0)))
    slope_t = jnp.broadcast_to(slopes[:, None, None], (N_HEADS, SUBLANES, LANES))
    kaug_g = kaug.reshape(N_HEADS, nt // KV_GROUP, KV_GROUP * MOBA_BLOCK, kaug.shape[3])
    ckt = cache_k.transpose(0, 1, 3, 4, 2)
    cvt = cache_v.transpose(0, 1, 3, 4, 2)
    o_t, o_s = _attention(qt, kbar_h, slope_t, kaug_g, vt, min(Q_TILE, KV_GROUP * MOBA_BLOCK),
                          page_table, ckt, cvt, 0, q_s, k_s, v_s, slopes)
    y_p = _l1_out(x1_p, o_t, sz_p, mod_p[1], norm_post[1], w_out1, MOBA_BLOCK, True)
    y_s = _l1_out(x1_s, o_s, sz_s, mod_s[1], norm_post[1], w_out1, ls, False)

    heads = lambda t: t.reshape(1, t.shape[0], t.shape[1], N_HEADS, HEAD_DIM)
    return (y_p, y_s, conv_p[None], conv_s[None],
            hre_p[None], him_p[None], hre_s[None], him_s[None],
            heads(k_p), heads(v_p), heads(k_s), heads(v_s))
```

```python
import functools

import jax
import jax.numpy as jnp
from jax import lax
from jax.experimental import pallas as pl
from jax.experimental.pallas import tpu as pltpu

F32 = jnp.float32
BF16 = jnp.bfloat16
HIGHEST = lax.Precision.HIGHEST

EPS = 1e-6
NEG_INF = -1e30
N_HEADS = 16
HEAD_DIM = 64
MOBA_BLOCK = 256
MOBA_TOPK = 3
PAGE_SIZE = 128
SSM_GROUP = 16
SSM_STATE = 64
MAX_BLOCKS = 64
KV_GROUP = 4
Q_TILE = 512
V_ROWS = 80
SAMPLE_BLOCKS_PER_STEP = 4
LOG2E = 1.4426950408889634
LANES = 128
SUBLANES = 8
VMEM_LIMIT = 56 * 1024 * 1024


def _cparams(*sem):
    return pltpu.CompilerParams(dimension_semantics=sem, vmem_limit_bytes=VMEM_LIMIT)


def _silu(x):
    return x * (1.0 / (1.0 + jnp.exp(-x)))


def _sigmoid(x):
    return 1.0 / (1.0 + jnp.exp(-x))


def _gelu_tanh(x):
    c = 0.7978845608028654
    return 0.5 * x * (1.0 + jnp.tanh(c * (x + 0.044715 * (x * x * x))))


def _rms(x, g):
    ms = jnp.mean(x * x, axis=-1, keepdims=True)
    return x * lax.rsqrt(ms + EPS) * g


def _adaln_kernel(c_ref, w_ref, b_ref, o_ref):
    c = c_ref[...]
    o_ref[0] = jnp.dot(_silu(c), w_ref[0], precision=HIGHEST,
                       preferred_element_type=F32) + b_ref[0]


def _adaln(c_all, ada_w, ada_b):
    depth, d, d3 = ada_w.shape
    r = c_all.shape[0]
    tn = 1024
    return pl.pallas_call(
        _adaln_kernel,
        grid=(depth, d3 // tn),
        in_specs=[pl.BlockSpec((r, d), lambda l, n: (0, 0)),
                  pl.BlockSpec((1, d, tn), lambda l, n: (l, 0, n)),
                  pl.BlockSpec((1, 1, tn), lambda l, n: (l, 0, n))],
        out_specs=pl.BlockSpec((1, r, tn), lambda l, n: (l, 0, n)),
        out_shape=jax.ShapeDtypeStruct((depth, r, d3), F32),
        compiler_params=_cparams("arbitrary", "arbitrary"),
    )(c_all, ada_w, ada_b.reshape(depth, 1, d3))


def _l0_in_kernel(x_ref, mod_ref, g_ref, w_ref, cw_ref, cb_ref, c0_ref,
                  ya_ref, u_ref, sz_ref, cbuf_ref, fbuf):
    ns, tl, d = x_ref.shape
    w = cw_ref.shape[1]

    @pl.when(pl.program_id(0) == 0)
    def _():
        fbuf[:, 6:8, :] = c0_ref[...]

    mod = mod_ref[...]
    shift, scale = mod[:, :, 0:d], mod[:, :, d:2 * d]
    h = _rms(x_ref[...], g_ref[...]) * (1.0 + scale) + shift
    proj = jnp.dot(h.reshape(ns * tl, d).astype(BF16), w_ref[...],
                   preferred_element_type=F32)
    xa, ba, ca = proj[:, 0:w], proj[:, w:2 * w], proj[:, 2 * w:3 * w]
    za, us, zs = proj[:, 3 * w:4 * w], proj[:, 4 * w:5 * w], proj[:, 5 * w:6 * w]
    f = (ca * xa).reshape(ns, tl, w)
    fbuf[:, 8:8 + tl, :] = f
    f1 = fbuf[:, 7:7 + tl, :]
    f2 = fbuf[:, 6:6 + tl, :]
    cw = cw_ref[...]
    conv = cb_ref[...] + cw[0:1] * f2 + cw[1:2] * f1 + cw[2:3] * f
    ya = ba.reshape(ns, tl, w) * conv * _silu(za).reshape(ns, tl, w)
    ya_ref[...] = ya
    u_ref[...] = us.reshape(ns, tl, w)
    sz_ref[...] = _silu(zs).reshape(ns, tl, w)
    tail = fbuf[:, tl + 6:tl + 8, :]
    cbuf_ref[...] = tail
    fbuf[:, 6:8, :] = tail


def _l0_in(x, mod, g, w_in, conv_w, conv_b, conv0, tl):
    b, l, d = x.shape
    w = conv_w.shape[1]
    ns = b
    return pl.pallas_call(
        _l0_in_kernel,
        grid=(l // tl,),
        in_specs=[pl.BlockSpec((ns, tl, d), lambda i: (0, i, 0)),
                  pl.BlockSpec((ns, 1, 3 * d), lambda i: (0, 0, 0)),
                  pl.BlockSpec((1, d), lambda i: (0, 0)),
                  pl.BlockSpec(w_in.shape, lambda i: (0, 0)),
                  pl.BlockSpec(conv_w.shape, lambda i: (0, 0)),
                  pl.BlockSpec((1, w), lambda i: (0, 0)),
                  pl.BlockSpec((ns, 2, w), lambda i: (0, 0, 0))],
        out_specs=[pl.BlockSpec((ns, tl, w), lambda i: (0, i, 0)),
                   pl.BlockSpec((ns, tl, w), lambda i: (0, i, 0)),
                   pl.BlockSpec((ns, tl, w), lambda i: (0, i, 0)),
                   pl.BlockSpec((ns, 2, w), lambda i: (0, 0, 0))],
        out_shape=[jax.ShapeDtypeStruct((b, l, w), F32),
                   jax.ShapeDtypeStruct((b, l, w), F32),
                   jax.ShapeDtypeStruct((b, l, w), F32),
                   jax.ShapeDtypeStruct((b, 2, w), F32)],
        scratch_shapes=[pltpu.VMEM((ns, tl + 8, w), F32)],
        compiler_params=_cparams("arbitrary"),
    )(x, mod, g.reshape(1, d), w_in, conv_w, conv_b.reshape(1, w), conv0)


def _s5_param_kernel(lr_ref, li_ref, ldt_ref, br_ref, bi_ref,
                     lbr_ref, lbi_ref, bbr_ref, bbi_ref):
    lr, li = lr_ref[...], li_ref[...]
    dt = jnp.exp(ldt_ref[...])
    mag = jnp.exp(lr * dt)
    ang = li * dt
    lbr = mag * jnp.cos(ang)
    lbi = mag * jnp.sin(ang)
    nr, ni = lbr - 1.0, lbi
    den = lr * lr + li * li
    cr = (nr * lr + ni * li) / den
    ci = (ni * lr - nr * li) / den
    br, bi = br_ref[...], bi_ref[...]
    lbr_ref[...] = lbr
    lbi_ref[...] = lbi
    bbr_ref[...] = cr * br - ci * bi
    bbi_ref[...] = cr * bi + ci * br


def _s5_params(lam_re, lam_im, log_dt, b_re, b_im):
    g, p = lam_re.shape
    hh = b_re.shape[2]
    rows = g * hh
    rep = lambda a: jnp.repeat(a, hh, axis=0)
    ldt = jnp.broadcast_to(log_dt[:, None], (g, p))
    br = b_re.transpose(0, 2, 1).reshape(rows, p)
    bi = b_im.transpose(0, 2, 1).reshape(rows, p)
    spec = pl.BlockSpec((rows, p), lambda: (0, 0))
    return pl.pallas_call(
        _s5_param_kernel,
        in_specs=[spec] * 5,
        out_specs=[spec] * 4,
        out_shape=[jax.ShapeDtypeStruct((rows, p), F32)] * 4,
    )(rep(lam_re), rep(lam_im), rep(ldt), br, bi)


def _s5_kernel(u_ref, h0_ref, lre_ref, lim_ref, wb_ref, wc_ref, d_ref,
               y_ref, hl_ref, hs, scr):
    tt = u_ref.shape[1]
    oct_w = wb_ref.shape[1]

    @pl.when(pl.program_id(1) == 0)
    def _():
        hs[...] = h0_ref[0]

    u = u_ref[0]
    ub = u.astype(BF16)
    nlt = scr.shape[0]
    for j in range(SUBLANES):
        o = j % 4
        bu = jnp.dot(ub[:, oct_w * o:oct_w * (o + 1)], wb_ref[j], preferred_element_type=F32)
        for c in range(nlt):
            scr[c, pl.ds(j, tt, stride=SUBLANES), :] = bu[:, LANES * c:LANES * (c + 1)]
    lre = [lre_ref[:, LANES * c:LANES * (c + 1)] for c in range(nlt)]
    lim = [lim_ref[:, LANES * c:LANES * (c + 1)] for c in range(nlt)]

    lre2 = [lre[c] * lre[c] - lim[c] * lim[c] for c in range(nlt)]
    lim2 = [2.0 * lre[c] * lim[c] for c in range(nlt)]

    def body(i, hc):
        r0 = pl.multiple_of(i * (2 * SUBLANES), 2 * SUBLANES)
        r1 = r0 + SUBLANES
        out = []
        for c in range(nlt):
            x0 = scr[c, pl.ds(r0, SUBLANES), :]
            x1 = scr[c, pl.ds(r1, SUBLANES), :]
            hr = pltpu.roll(hc[c], 4, 0)
            c01 = lre[c] * x0 + lim[c] * pltpu.roll(x0, 4, 0) + x1
            h1 = lre[c] * hc[c] + lim[c] * hr + x0
            h2 = lre2[c] * hc[c] + lim2[c] * hr + c01
            scr[c, pl.ds(r0, SUBLANES), :] = h1
            scr[c, pl.ds(r1, SUBLANES), :] = h2
            out.append(h2)
        return tuple(out)

    h0 = tuple(hs[:, LANES * c:LANES * (c + 1)] for c in range(nlt))
    hc = lax.fori_loop(0, tt // 2, body, h0, unroll=4)
    h = jnp.concatenate(hc, axis=1)
    hs[...] = h
    hl_ref[0] = h
    ys = []
    for o in range(4):
        parts = [scr[c, pl.ds(part + o, tt, stride=SUBLANES), :]
                 for part in (0, 4) for c in range(nlt)]
        hcat = jnp.concatenate(parts, axis=1).astype(BF16)
        ys.append(jnp.dot(hcat, wc_ref[o], preferred_element_type=F32))
    y_ref[0] = jnp.concatenate(ys, axis=1) + d_ref[...] * u


def _s5(u, h0_tiles, lre, lim, wb, wc, d, tt):
    b, l, w = u.shape
    sw = lre.shape[1]
    return pl.pallas_call(
        _s5_kernel,
        grid=(b, l // tt),
        in_specs=[pl.BlockSpec((1, tt, w), lambda s, t: (s, t, 0)),
                  pl.BlockSpec((1, SUBLANES, sw), lambda s, t: (s, 0, 0)),
                  pl.BlockSpec(lre.shape, lambda s, t: (0, 0)),
                  pl.BlockSpec(lim.shape, lambda s, t: (0, 0)),
                  pl.BlockSpec(wb.shape, lambda s, t: (0, 0, 0)),
                  pl.BlockSpec(wc.shape, lambda s, t: (0, 0, 0)),
                  pl.BlockSpec((1, w), lambda s, t: (0, 0))],
        out_specs=[pl.BlockSpec((1, tt, w), lambda s, t: (s, t, 0)),
                   pl.BlockSpec((1, SUBLANES, sw), lambda s, t: (s, 0, 0))],
        out_shape=[jax.ShapeDtypeStruct((b, l, w), F32),
                   jax.ShapeDtypeStruct((b, SUBLANES, sw), F32)],
        scratch_shapes=[pltpu.VMEM((SUBLANES, sw), F32),
                        pltpu.VMEM((sw // LANES, tt * SUBLANES, LANES), F32)],
        compiler_params=_cparams("arbitrary", "arbitrary"),
    )(u, h0_tiles, lre, lim, wb, wc, d.reshape(1, w))


def _l0_out_kernel(x_ref, ya_ref, ys_ref, sz_ref, mod_ref, g_ref, glu_ref, w_ref, o_ref):
    ns, tl, d = x_ref.shape
    w = ya_ref.shape[2]
    rows = ns * tl
    g1 = _gelu_tanh(ys_ref[...].reshape(rows, w))
    lin = jnp.dot(g1.astype(BF16), glu_ref[...], preferred_element_type=F32)
    ys = g1 * _sigmoid(lin) * sz_ref[...].reshape(rows, w)
    cat = jnp.concatenate([ya_ref[...].reshape(rows, w).astype(BF16), ys.astype(BF16)], axis=1)
    out = jnp.dot(cat, w_ref[...], preferred_element_type=F32).reshape(ns, tl, d)
    gate = mod_ref[...][:, :, 2 * d:3 * d]
    o_ref[...] = x_ref[...] + (1.0 + gate) * _rms(out, g_ref[...])


def _l0_out(x, ya, ys, sz, mod, g, glu_w, w_out, tl):
    b, l, d = x.shape
    w = ya.shape[2]
    row = lambda width: pl.BlockSpec((b, tl, width), lambda i: (0, i, 0))
    return pl.pallas_call(
        _l0_out_kernel,
        grid=(l // tl,),
        in_specs=[row(d), row(w), row(w), row(w),
                  pl.BlockSpec((b, 1, 3 * d), lambda i: (0, 0, 0)),
                  pl.BlockSpec((1, d), lambda i: (0, 0)),
                  pl.BlockSpec(glu_w.shape, lambda i: (0, 0)),
                  pl.BlockSpec(w_out.shape, lambda i: (0, 0))],
        out_specs=row(d),
        out_shape=jax.ShapeDtypeStruct((b, l, d), F32),
        compiler_params=_cparams("arbitrary"),
    )(x, ya, ys, sz, mod, g.reshape(1, d), glu_w, w_out)


def _l1_in_kernel(prompt, x_ref, mod_ref, g_ref, w_ref, k_ref, v_ref, sz_ref, *rest):
    ns, tl, d = x_ref.shape
    rows = ns * tl
    mod = mod_ref[...]
    shift, scale = mod[:, :, 0:d], mod[:, :, d:2 * d]
    h = _rms(x_ref[...], g_ref[...]) * (1.0 + scale) + shift
    proj = jnp.dot(h.reshape(rows, d).astype(BF16), w_ref[...], preferred_element_type=F32)
    q, k, v, z = (proj[:, i * d:(i + 1) * d] for i in range(4))
    k_ref[...] = k.reshape(ns, tl, d)
    v_ref[...] = v.reshape(ns, tl, d)
    sz_ref[...] = _silu(z).reshape(ns, tl, d)
    if not prompt:
        (q_ref,) = rest
        q_ref[...] = q.reshape(ns, tl, d)
        return
    qt_ref, kaug_ref, vt_ref, kbar_ref = rest
    i = pl.program_id(0)
    qt_ref[...] = q.T.reshape(N_HEADS, HEAD_DIM, rows)
    ones_rows = (lax.broadcasted_iota(jnp.int32, (N_HEADS, V_ROWS - HEAD_DIM, rows), 1) == 0)
    vt = jnp.concatenate([v.T.reshape(N_HEADS, HEAD_DIM, rows), ones_rows.astype(F32)], axis=1)
    vt_ref[...] = vt.reshape(N_HEADS, 1, V_ROWS, rows).astype(BF16)
    kbar_ref[0] = jnp.mean(k, axis=0, keepdims=True)
    lane = lax.broadcasted_iota(jnp.int32, (rows, LANES), 1)
    rowi = lax.broadcasted_iota(jnp.int32, (rows, LANES), 0)
    onehot = (lane - HEAD_DIM == i).astype(F32)
    tile1 = jnp.where(lane < MAX_BLOCKS, (lane == i).astype(F32),
                      jnp.where(lane < MAX_BLOCKS + 2, rowi.astype(F32), 0.0)).astype(BF16)
    for hd in range(N_HEADS):
        kt = k[:, LANES * (hd // 2):LANES * (hd // 2 + 1)]
        if hd % 2:
            kt = pltpu.roll(kt, HEAD_DIM, 1)
        tile0 = jnp.where(lane < HEAD_DIM, kt, onehot).astype(BF16)
        kaug_ref[hd, 0] = jnp.concatenate([tile0, tile1], axis=1)


def _l1_in(x, mod, g, w_in, tl, prompt):
    b, l, d = x.shape
    nt = l // tl
    row = pl.BlockSpec((b, tl, d), lambda i: (0, i, 0))
    out_specs = [row, row, row]
    out_shape = [jax.ShapeDtypeStruct((b, l, d), F32)] * 3
    if prompt:
        assert b == 1 and tl == MOBA_BLOCK and nt <= MAX_BLOCKS and nt % KV_GROUP == 0
        out_specs += [pl.BlockSpec((N_HEADS, HEAD_DIM, tl), lambda i: (0, 0, i)),
                      pl.BlockSpec((N_HEADS, 1, tl, 2 * LANES), lambda i: (0, i, 0, 0)),
                      pl.BlockSpec((N_HEADS, 1, V_ROWS, tl),
                                   lambda i: (0, i // KV_GROUP, 0, i % KV_GROUP)),
                      pl.BlockSpec((1, 1, d), lambda i: (i, 0, 0))]
        out_shape += [jax.ShapeDtypeStruct((N_HEADS, HEAD_DIM, l), F32),
                      jax.ShapeDtypeStruct((N_HEADS, nt, tl, 2 * LANES), BF16),
                      jax.ShapeDtypeStruct((N_HEADS, nt // KV_GROUP, V_ROWS, KV_GROUP * tl), BF16),
                      jax.ShapeDtypeStruct((nt, 1, d), F32)]
    else:
        out_specs += [row]
        out_shape += [jax.ShapeDtypeStruct((b, l, d), F32)]
    return pl.pallas_call(
        functools.partial(_l1_in_kernel, prompt),
        grid=(nt,),
        in_specs=[row,
                  pl.BlockSpec((b, 1, 3 * d), lambda i: (0, 0, 0)),
                  pl.BlockSpec((1, d), lambda i: (0, 0)),
                  pl.BlockSpec(w_in.shape, lambda i: (0, 0))],
        out_specs=out_specs,
        out_shape=out_shape,
        compiler_params=_cparams("arbitrary"),
    )(x, mod, g.reshape(1, d), w_in)


def _select_topk(gate, n_past):
    nb = gate.shape[0]
    jidx = lax.broadcasted_iota(jnp.int32, gate.shape, 0)
    jf = jidx.astype(F32)
    past = jidx < n_past
    gm = jnp.where(past, gate, -jnp.inf)
    sel = None
    for _ in range(MOBA_TOPK):
        m = jnp.max(gm, axis=0, keepdims=True)
        idx = jnp.min(jnp.where(gm == m, jf, float(nb)), axis=0, keepdims=True)
        pick = jf == idx
        sel = pick if sel is None else jnp.logical_or(sel, pick)
        gm = jnp.where(pick, -jnp.inf, gm)
    return jnp.logical_and(sel, past)


def _split_bf16(a):
    hi = a.astype(BF16)
    lo = (a - hi.astype(F32)).astype(BF16)
    return hi, lo


def _query_features(qt, kbar, slope, tile):
    tq = qt.shape[1]
    slope = slope * LOG2E
    gate = jnp.dot(kbar, qt, precision=HIGHEST, preferred_element_type=F32)
    jidx = lax.broadcasted_iota(jnp.int32, gate.shape, 0)
    lane = lax.broadcasted_iota(jnp.int32, gate.shape, 1)
    blk = tile * (tq // MOBA_BLOCK) + lane // MOBA_BLOCK
    sel = _select_topk(gate, blk)
    bias = -(slope * float(MOBA_BLOCK)) * (blk - jidx).astype(F32)
    add = jnp.where(sel, bias, jnp.where(jidx == blk, 0.0, NEG_INF))
    a_hi, a_lo = _split_bf16(add)
    s_hi, s_lo = _split_bf16(slope)
    tail = jnp.where(jidx == 0, s_hi.astype(F32), jnp.where(jidx == 1, s_lo.astype(F32), 0.0))
    qs = qt * (HEAD_DIM ** -0.5 * LOG2E)
    qa = jnp.concatenate([qs, a_hi.astype(F32), a_lo.astype(F32), tail], axis=0)
    return qa.astype(BF16)


def _attn_body(tile, qt_ref, kbar_ref, slope_ref, kaug_ref, vt_ref, o_ref, s_a, s_b):
    tq = qt_ref.shape[2]
    gk = kaug_ref.shape[2]
    slope = slope_ref[0][0:1, 0:1]
    qa = _query_features(qt_ref[0], kbar_ref[0], slope, tile)
    last = lax.div(tile * tq, gk)

    def qk(g, dst):
        dst[...] = jnp.dot(kaug_ref[0, g], qa, preferred_element_type=F32)

    def update(src, g, carry, causal):
        m, acc = carry
        s = src[...]
        if causal:
            kpos = g * gk + lax.broadcasted_iota(jnp.int32, s.shape, 0)
            qpos = tile * tq + lax.broadcasted_iota(jnp.int32, s.shape, 1)
            s = jnp.where(kpos <= qpos, s, NEG_INF)
        m_new = jnp.maximum(m, jnp.max(s, axis=0, keepdims=True))
        p = jnp.exp2(s - m_new).astype(BF16)
        acc = jnp.exp2(m - m_new) * acc + jnp.dot(vt_ref[0, g], p, preferred_element_type=F32)
        return m_new, acc

    def pair(t, carry):
        g = 2 * t
        qk(g + 1, s_b)
        carry = update(s_a, g, carry, False)
        qk(g + 2, s_a)
        return update(s_b, g + 1, carry, False)

    def tail_even(carry):
        return update(s_a, last, carry, True)

    def tail_odd(carry):
        qk(last, s_b)
        carry = update(s_a, last - 1, carry, False)
        return update(s_b, last, carry, True)

    init = (jnp.full((1, tq), -jnp.inf, F32), jnp.zeros((vt_ref.shape[2], tq), F32))
    qk(0, s_a)
    carry = lax.fori_loop(0, last // 2, pair, init)
    _, acc = lax.cond(last % 2 == 0, tail_even, tail_odd, carry)
    o_ref[0] = acc[0:HEAD_DIM] / acc[HEAD_DIM:HEAD_DIM + 1]


def _fold_heads(pv):
    rows, d = pv.shape
    r = lax.broadcasted_iota(jnp.int32, (rows, LANES), 0) // SUBLANES
    c = lax.broadcasted_iota(jnp.int32, (rows, LANES), 1) // HEAD_DIM
    out = jnp.zeros((rows, LANES), F32)
    for t in range(d // LANES):
        out = out + jnp.where(r == 2 * t + c, pv[:, LANES * t:LANES * (t + 1)], 0.0)
    return out


def _sample_body(bps, step, k_refs, v_refs, q_ref, kn_ref, vn_ref, slope_c_ref, qoff_c_ref,
                 slope_r_ref, qoff_r_ref, o_ref, g_s, m_s, l_s, o_s, qs_s, qf_s):
    ppb = MOBA_BLOCK // PAGE_SIZE
    nb = g_s.shape[0]
    n, d = qs_s.shape
    slope_c = slope_c_ref[...]
    lane_n = lax.broadcasted_iota(jnp.int32, (n, LANES), 1)

    @pl.when(step == 0)
    def _():
        m_s[...] = jnp.zeros(m_s.shape, F32)
        l_s[...] = jnp.zeros(l_s.shape, F32)
        q_rep = jnp.concatenate([q_ref[0]] * (n // q_ref.shape[1]), axis=0)
        row_h = lax.broadcasted_iota(jnp.int32, (n, d), 0) // q_ref.shape[1]
        col_h = lax.broadcasted_iota(jnp.int32, (n, d), 1) // HEAD_DIM
        q_bd = jnp.where(row_h == col_h, q_rep, 0.0)
        qs_s[...] = (q_bd * (HEAD_DIM ** -0.5)).astype(BF16)
        qf_s[...] = q_bd.T

    qs = qs_s[...]

    def put_column(ref, col, idx):
        ref[...] = jnp.where(lane_n == idx, col, ref[...])

    def partial(s):
        m = jnp.max(s, axis=1, keepdims=True)
        p = jnp.exp(s - m)
        return m, jnp.sum(p, axis=1, keepdims=True), p.astype(BF16)

    def pages(page_refs, blk):
        return jnp.concatenate([page_refs[ppb * blk + o][0, 0].reshape(d, PAGE_SIZE)
                                for o in range(ppb)], axis=1)

    cpos = lax.broadcasted_iota(jnp.int32, (1, MOBA_BLOCK), 1).astype(F32)
    m_all, l_all = m_s[...], l_s[...]
    kts = [pages(k_refs, blk) for blk in range(bps)]
    ss = [jnp.dot(qs, kt.astype(BF16), preferred_element_type=F32) + slope_c * cpos for kt in kts]
    for blk in range(bps):
        j = step * bps + blk
        ksum = jnp.sum(kts[blk], axis=1, keepdims=True)
        g_s[pl.ds(j, 1), :] = jnp.sum(ksum * qf_s[...], axis=0, keepdims=True) * (1.0 / MOBA_BLOCK)
    ps = []
    for blk in range(bps):
        j = step * bps + blk
        m, l, p = partial(ss[blk])
        m_all = jnp.where(lane_n == j, m, m_all)
        l_all = jnp.where(lane_n == j, l, l_all)
        ps.append(p)
    for blk in range(bps):
        j = step * bps + blk
        pv = lax.dot_general(ps[blk], pages(v_refs, blk).astype(BF16), (((1,), (1,)), ((), ())),
                             preferred_element_type=F32)
        o_s[j] = _fold_heads(pv)
    m_s[...] = m_all
    l_s[...] = l_all

    @pl.when(step == nb // bps - 1)
    def _():
        nq = kn_ref.shape[1]
        pad = jnp.zeros((LANES - nq, d), F32)
        kn = jnp.concatenate([kn_ref[0], pad], axis=0).astype(BF16)
        vn = jnp.concatenate([vn_ref[0], pad], axis=0).astype(BF16)
        kpos = lax.broadcasted_iota(jnp.int32, (1, LANES), 1).astype(F32)
        qoff_c = qoff_c_ref[...]
        s_own = lax.dot_general(qs, kn, (((1,), (1,)), ((), ())), preferred_element_type=F32) \
            - slope_c * (qoff_c - kpos)
        s_own = jnp.where(kpos <= qoff_c, s_own, NEG_INF)
        m_own, l_own, p_own = partial(s_own)
        put_column(m_s, m_own, nb)
        put_column(l_s, l_own, nb)
        o_own = _fold_heads(jnp.dot(p_own, vn, preferred_element_type=F32))
        m_r = m_s[...].T
        l_r = l_s[...].T
        gate = jnp.concatenate([g_s[...], jnp.zeros((LANES - nb, n), F32)], axis=0)
        jidx = lax.broadcasted_iota(jnp.int32, (LANES, n), 0)
        sel = jnp.logical_or(_select_topk(gate, nb), jidx == nb)
        past_len = float(nb * MOBA_BLOCK)
        bias = slope_r_ref[...] * (past_len + qoff_r_ref[...] - (jidx * MOBA_BLOCK).astype(F32))
        mj = jnp.where(sel, m_r - jnp.where(jidx == nb, 0.0, bias), -jnp.inf)
        m_tot = jnp.max(mj, axis=0, keepdims=True)
        w = jnp.where(sel, jnp.exp(mj - m_tot), 0.0)
        l_tot = jnp.sum(w * l_r, axis=0, keepdims=True)
        w_t = (w / l_tot).T

        def column(jj):
            return jnp.sum(jnp.where(lane_n == jj, w_t, 0.0), axis=1, keepdims=True)

        acc = lax.fori_loop(0, nb, lambda jj, a: a + column(jj) * o_s[jj],
                            column(nb) * o_own, unroll=SUBLANES)
        lane = lax.broadcasted_iota(jnp.int32, (nq, LANES), 1)
        tiles = []
        for t in range(N_HEADS // 2):
            ev = acc[SUBLANES * 2 * t:SUBLANES * (2 * t + 1), :]
            od = acc[SUBLANES * (2 * t + 1):SUBLANES * (2 * t + 2), :]
            tiles.append(jnp.where(lane < HEAD_DIM, ev, od))
        o_ref[0] = jnp.concatenate(tiles, axis=1)


N_PROMPT_IN = 5


def _attention_kernel(bps, nt, ns, pt_ref, *refs):
    npg = bps * (MOBA_BLOCK // PAGE_SIZE)
    p_in = refs[0:N_PROMPT_IN]
    rest = refs[N_PROMPT_IN:]
    k_refs, v_refs = rest[0:npg], rest[npg:2 * npg]
    s_in = rest[2 * npg:2 * npg + 7]
    o_t_ref, o_s_ref, s_a, s_b = rest[2 * npg + 7:2 * npg + 11]
    s_scratch = rest[2 * npg + 11:]
    p = pl.program_id(0)
    _attn_body(lax.rem(p, nt), *p_in, o_t_ref, s_a, s_b)
    _sample_body(bps, lax.rem(p, ns), k_refs, v_refs, *s_in, o_s_ref, *s_scratch)


def _attention(qt, kbar_h, slope_t, kaug_g, vt_g, tq,
               page_table, cache_kt, cache_vt, layer, q, k_new, v_new, slopes):
    nh, hd, l = qt.shape
    ng, gk, f = kaug_g.shape[1:]
    assert gk % tq == 0 and l % tq == 0
    nt = l // tq
    b, n_pages = page_table.shape
    ppb = MOBA_BLOCK // PAGE_SIZE
    nb = n_pages // ppb
    assert n_pages % ppb == 0 and ppb == 2 and nb < LANES and nb % SUBLANES == 0
    lq, d = k_new.shape[1], k_new.shape[2]
    n = N_HEADS * lq
    assert lq == SUBLANES and n == LANES
    bps = SAMPLE_BLOCKS_PER_STEP if nb % SAMPLE_BLOCKS_PER_STEP == 0 else 1
    ns = nb // bps
    npg = bps * ppb
    assert nh * nt == b * ns, "prompt tiles and sample steps must pair up one to one"
    slope_row = jnp.repeat(slopes, lq)[None, :]
    qoff_row = jnp.tile(jnp.arange(lq, dtype=F32), N_HEADS)[None, :]

    once = pl.Buffered(1)
    prompt_specs = [
        pl.BlockSpec((1, hd, tq), lambda p, pt: (p // nt, 0, p % nt)),
        pl.BlockSpec((1, MAX_BLOCKS, hd), lambda p, pt: (p // nt, 0, 0)),
        pl.BlockSpec((1, SUBLANES, LANES), lambda p, pt: (p // nt, 0, 0)),
        pl.BlockSpec((1, ng, gk, f), lambda p, pt: (p // nt, 0, 0, 0)),
        pl.BlockSpec((1, ng, vt_g.shape[2], gk), lambda p, pt: (p // nt, 0, 0, 0),
                     pipeline_mode=once)]
    assert len(prompt_specs) == N_PROMPT_IN
    page = lambda off: pl.BlockSpec(
        (1, 1, N_HEADS, HEAD_DIM, PAGE_SIZE),
        lambda p, pt: (layer, pt[p // ns, npg * (p % ns) + off], 0, 0, 0))
    per_seq = lambda r, c: pl.BlockSpec((1, r, c), lambda p, pt: (p // ns, 0, 0))
    col = pl.BlockSpec((n, 1), lambda p, pt: (0, 0))
    row = pl.BlockSpec((1, n), lambda p, pt: (0, 0))
    pages = [page(off) for off in range(npg)]
    grid_spec = pltpu.PrefetchScalarGridSpec(
        num_scalar_prefetch=1,
        grid=(nh * nt,),
        in_specs=prompt_specs + pages + pages + [
            per_seq(lq, d), per_seq(lq, d), per_seq(lq, d), col, col, row, row],
        out_specs=[pl.BlockSpec((1, hd, tq), lambda p, pt: (p // nt, 0, p % nt)),
                   per_seq(lq, d)],
        scratch_shapes=[pltpu.VMEM((gk, tq), F32), pltpu.VMEM((gk, tq), F32),
                        pltpu.VMEM((nb, n), F32),
                        pltpu.VMEM((n, LANES), F32),
                        pltpu.VMEM((n, LANES), F32),
                        pltpu.VMEM((nb, n, LANES), F32),
                        pltpu.VMEM((n, d), BF16),
                        pltpu.VMEM((d, n), F32)],
    )
    return pl.pallas_call(
        functools.partial(_attention_kernel, bps, nt, ns),
        grid_spec=grid_spec,
        out_shape=[jax.ShapeDtypeStruct((nh, hd, l), F32),
                   jax.ShapeDtypeStruct((b, lq, d), F32)],
        compiler_params=_cparams("arbitrary"),
    )(page_table, qt, kbar_h, slope_t, kaug_g, vt_g,
      *([cache_kt] * npg), *([cache_vt] * npg), q, k_new, v_new,
      slope_row.T, qoff_row.T, slope_row, qoff_row)


def _l1_out_kernel(transposed, x_ref, o_ref, sz_ref, mod_ref, g_ref, w_ref, y_ref):
    ns, tl, d = x_ref.shape
    rows = ns * tl
    if transposed:
        o = o_ref[...].reshape(d, rows).T
    else:
        o = o_ref[...].reshape(rows, d)
    a = (o * sz_ref[...].reshape(rows, d)).astype(BF16)
    out = jnp.dot(a, w_ref[...], preferred_element_type=F32).reshape(ns, tl, d)
    gate = mod_ref[...][:, :, 2 * d:3 * d]
    y_ref[...] = x_ref[...] + (1.0 + gate) * _rms(out, g_ref[...])


def _l1_out(x, o, sz, mod, g, w_out, tl, transposed):
    b, l, d = x.shape
    row = pl.BlockSpec((b, tl, d), lambda i: (0, i, 0))
    if transposed:
        o_spec = pl.BlockSpec((N_HEADS, HEAD_DIM, tl), lambda i: (0, 0, i))
    else:
        o_spec = row
    return pl.pallas_call(
        functools.partial(_l1_out_kernel, transposed),
        grid=(l // tl,),
        in_specs=[row, o_spec, row,
                  pl.BlockSpec((b, 1, 3 * d), lambda i: (0, 0, 0)),
                  pl.BlockSpec((1, d), lambda i: (0, 0)),
                  pl.BlockSpec(w_out.shape, lambda i: (0, 0))],
        out_specs=row,
        out_shape=jax.ShapeDtypeStruct((b, l, d), F32),
        compiler_params=_cparams("arbitrary"),
    )(x, o, sz, mod, g.reshape(1, d), w_out)


def _s5_weights(lbr, lbi, bbr, bbi, c_re, c_im):
    g = c_re.shape[0]
    hh, p = SSM_GROUP, SSM_STATE
    oct_n = g // 8
    lam_r = lbr.reshape(g, hh, p)[:, 0, :].reshape(oct_n, 8 * p)
    lam_i = lbi.reshape(g, hh, p)[:, 0, :].reshape(oct_n, 8 * p)
    lre = jnp.concatenate([lam_r, lam_r], axis=0)
    lim = jnp.concatenate([-lam_i, lam_i], axis=0)
    eye = jnp.eye(8, dtype=F32)

    def in_w(bb):
        bb = bb.reshape(oct_n, 8, hh, p)
        return jnp.einsum('oghp,gk->oghkp', bb, eye).reshape(oct_n, 8 * hh, 8 * p)

    def out_w(cc):
        cc = cc.reshape(oct_n, 8, hh, p)
        return jnp.einsum('oghp,gk->ogpkh', cc, eye).reshape(oct_n, 8 * p, 8 * hh)

    wb = jnp.concatenate([in_w(bbr), in_w(bbi)], axis=0).astype(BF16)
    wc = jnp.concatenate([out_w(c_re), -out_w(c_im)], axis=1).astype(BF16)
    return lre, lim, wb, wc


def _state_tiles(re, im):
    b = re.shape[0]
    return jnp.concatenate([re.reshape(b, 4, -1), im.reshape(b, 4, -1)], axis=1)


def kernel(x_prompt, x_sample, state_conv, state_ssm_re, state_ssm_im, cache_k, cache_v, page_table,
           c_prompt, c_sample, norm_pre, norm_post, ada_w, ada_b, w_in_even, conv_w, conv_b,
           ssm_lambda_re, ssm_lambda_im, ssm_log_dt, ssm_b_re, ssm_b_im, ssm_c_re, ssm_c_im,
           ssm_d, ssm_glu_w, w_out_even, w_in_odd, w_out_odd):
    bp, lp, d = x_prompt.shape
    bs, ls, _ = x_sample.shape
    g, p = ssm_lambda_re.shape[1], ssm_lambda_re.shape[2]
    n_pool = cache_k.shape[1]
    assert bp == 1 and ls == SUBLANES and d == N_HEADS * HEAD_DIM and g == 32 and p == SSM_STATE

    n_c = bp + bs
    c_all = jnp.concatenate(
        [c_prompt, c_sample, jnp.zeros((-n_c % SUBLANES, d), F32)], axis=0)
    mod = _adaln(c_all, ada_w, ada_b)
    mod_p = [mod[l, 0:bp][:, None, :] for l in range(2)]
    mod_s = [mod[l, bp:n_c][:, None, :] for l in range(2)]

    lbr, lbi, bbr, bbi = _s5_params(ssm_lambda_re[0], ssm_lambda_im[0], ssm_log_dt[0],
                                    ssm_b_re[0], ssm_b_im[0])
    lre, lim, wb, wc = _s5_weights(lbr, lbi, bbr, bbi, ssm_c_re[0], ssm_c_im[0])
    w_in0 = w_in_even[0].astype(BF16)
    glu_w = ssm_glu_w[0].astype(BF16)
    w_out0 = w_out_even[0].astype(BF16)
    wcv = conv_w.shape[2]

    def layer0(x, mods, conv0, h0_tiles, tl, tt):
        ya, u, sz, cbuf = _l0_in(x, mods, norm_pre[0], w_in0, conv_w[0], conv_b[0], conv0, tl)
        ys, h_last = _s5(u, h0_tiles, lre, lim, wb, wc, ssm_d[0], tt)
        x1 = _l0_out(x, ya, ys, sz, mods, norm_post[0], glu_w, w_out0, tl)
        b = x.shape[0]
        h_re = h_last[:, 0:4].reshape(b, g, p)
        h_im = h_last[:, 4:8].reshape(b, g, p)
        return x1, cbuf, h_re, h_im

    x1_p, conv_p, hre_p, him_p = layer0(
        x_prompt, mod_p[0], jnp.zeros((bp, 2, wcv), F32),
        jnp.zeros((bp, SUBLANES, 2 * g * p // SUBLANES), F32), 512, 256)
    x1_s, conv_s, hre_s, him_s = layer0(
        x_sample, mod_s[0], state_conv[0],
        _state_tiles(state_ssm_re[0], state_ssm_im[0]), ls, ls)

    w_in1 = w_in_odd[0].astype(BF16)
    w_out1 = w_out_odd[0].astype(BF16)
    slopes = jnp.exp2(-8.0 * jnp.arange(1, N_HEADS + 1, dtype=F32) / N_HEADS)

    k_p, v_p, sz_p, qt, kaug, vt, kbar = _l1_in(x1_p, mod_p[1], norm_pre[1], w_in1, MOBA_BLOCK, True)
    k_s, v_s, sz_s, q_s = _l1_in(x1_s, mod_s[1], norm_pre[1], w_in1, ls, False)
    nt = lp // MOBA_BLOCK
    kbar_h = kbar.reshape(nt, N_HEADS, HEAD_DIM).transpose(1, 0, 2)
    kbar_h = jnp.pad(kbar_h, ((0, 0), (0, MAX_BLOCKS - nt), (0, 0)))
    slope_t = jnp.broadcast_to(slopes[:, None, None], (N_HEADS, SUBLANES, LANES))
    kaug_g = kaug.reshape(N_HEADS, nt // KV_GROUP, KV_GROUP * MOBA_BLOCK, kaug.shape[3])
    ckt = cache_k.transpose(0, 1, 3, 4, 2)
    cvt = cache_v.transpose(0, 1, 3, 4, 2)
    o_t, o_s = _attention(qt, kbar_h, slope_t, kaug_g, vt, min(Q_TILE, KV_GROUP * MOBA_BLOCK),
                          page_table, ckt, cvt, 0, q_s, k_s, v_s, slopes)
    y_p = _l1_out(x1_p, o_t, sz_p, mod_p[1], norm_post[1], w_out1, MOBA_BLOCK, True)
    y_s = _l1_out(x1_s, o_s, sz_s, mod_s[1], norm_post[1], w_out1, ls, False)

    heads = lambda t: t.reshape(1, t.shape[0], t.shape[1], N_HEADS, HEAD_DIM)
    return (y_p, y_s, conv_p[None], conv_s[None],
            hre_p[None], him_p[None], hre_s[None], him_s[None],
            heads(k_p), heads(v_p), heads(k_s), heads(v_s))
```
